```python
import jax, jax.numpy as jnp
from jax import lax
import numpy as np

D_MODEL = 4096
BATCH = 1
SEQ = 8192
DEPTH = 4
DEC_BATCH = 8
DEC_SEQ = 16
PAST_LEN = 1024

CHUNK = 64
EPS = 1e-6
D_PLE = 256
D_FF = 11008
MLA_HEADS = 16
MLA_Q_RANK = 1024
MLA_KV_RANK = 512
MLA_NOPE = 128
MLA_ROPE = 64
MLA_V = 128
ROPE_THETA = 10000.0
QBLOCK = 128
CONV_CH = 1024
CONV_WIDTH = 31
GLA_HEADS = 4
GLA_DK = 128
GLA_DV = 256
GLA_GATE_RANK = 16
GLA_TAU = 16.0

D_MLA_IN = MLA_Q_RANK + MLA_KV_RANK + MLA_ROPE
D_CONV_IN = 2 * CONV_CH
D_GLA_IN = GLA_HEADS * (2 * GLA_DK + 2 * GLA_DV) + GLA_GATE_RANK
D_IN = D_MLA_IN + D_CONV_IN + D_GLA_IN
D_MIX = MLA_HEADS * MLA_V + CONV_CH + GLA_HEADS * GLA_DV

kernel_name = "hymba_mla_conformer_gla_streaming_step"


def rmsnorm(x, g):
    xf = x.astype(jnp.float32)
    y = xf * lax.rsqrt(jnp.mean(xf * xf, axis=-1, keepdims=True) + EPS)
    return (y * g.astype(jnp.float32)).astype(x.dtype)


def layernorm(x, g, b):
    xf = x.astype(jnp.float32)
    xc = xf - jnp.mean(xf, axis=-1, keepdims=True)
    y = xc * lax.rsqrt(jnp.mean(xc * xc, axis=-1, keepdims=True) + EPS)
    return (y * g.astype(jnp.float32) + b.astype(jnp.float32)).astype(x.dtype)


def apply_rope(x, pos):
    half = x.shape[-1] // 2
    inv_freq = ROPE_THETA ** (-jnp.arange(half, dtype=jnp.float32) / half)
    ang = pos.astype(jnp.float32)[:, None] * inv_freq[None, :]
    cos = jnp.cos(ang)[:, None, :]
    sin = jnp.sin(ang)[:, None, :]
    xf = x.astype(jnp.float32)
    x1, x2 = xf[..., :half], xf[..., half:]
    return jnp.concatenate([x1 * cos - x2 * sin, x2 * cos + x1 * sin], axis=-1).astype(x.dtype)


def mla_mixer(z, pos, past_ckv, past_krope, q_norm, kv_norm, w_uq, w_uk, w_uv):
    B, L, _ = z.shape
    P = past_ckv.shape[1]
    c_q = z[..., :MLA_Q_RANK]
    c_kv = z[..., MLA_Q_RANK:MLA_Q_RANK + MLA_KV_RANK]
    k_r = z[..., MLA_Q_RANK + MLA_KV_RANK:]
    q = (rmsnorm(c_q, q_norm) @ w_uq).reshape(B, L, MLA_HEADS, MLA_NOPE + MLA_ROPE)
    q_nope = q[..., :MLA_NOPE]
    q_rope = apply_rope(q[..., MLA_NOPE:], pos)
    c_kv = rmsnorm(c_kv, kv_norm)
    k_r = apply_rope(k_r[:, :, None, :], pos)[:, :, 0]
    ckv_all = jnp.concatenate([past_ckv, c_kv], axis=1)
    kr_all = jnp.concatenate([past_krope, k_r], axis=1)
    k_pos = jnp.concatenate([jnp.arange(P, dtype=jnp.int32), pos])
    k_nope = jnp.einsum("bkr,rhd->bkhd", ckv_all, w_uk)
    v = jnp.einsum("bkr,rhd->bkhd", ckv_all, w_uv)
    qb = QBLOCK if L % QBLOCK == 0 else L
    nb = L // qb
    qn_blk = q_nope.reshape(B, nb, qb, MLA_HEADS, MLA_NOPE).transpose(1, 0, 2, 3, 4)
    qr_blk = q_rope.reshape(B, nb, qb, MLA_HEADS, MLA_ROPE).transpose(1, 0, 2, 3, 4)
    qp_blk = pos.reshape(nb, qb)
    k_chunk = k_pos // CHUNK
    scale = (MLA_NOPE + MLA_ROPE) ** -0.5

    def attend_block(args):
        qn_b, qr_b, qp_b = args
        s = (jnp.einsum("bqhd,bkhd->bhqk", qn_b, k_nope)
             + jnp.einsum("bqhd,bkd->bhqk", qr_b, kr_all)).astype(jnp.float32) * scale
        visible = k_chunk[None, :] <= (qp_b // CHUNK)[:, None]
        s = jnp.where(visible[None, None], s, -jnp.inf)
        w = jax.nn.softmax(s, axis=-1).astype(v.dtype)
        return jnp.einsum("bhqk,bkhd->bqhd", w, v)

    o = lax.map(attend_block, (qn_blk, qr_blk, qp_blk))
    o = o.transpose(1, 0, 2, 3, 4).reshape(B, L, MLA_HEADS * MLA_V)
    return o, c_kv, k_r


def conv_mixer(z, past_u, dw_w, dw_b, ln_g, ln_b):
    u = z[..., :CONV_CH] * jax.nn.sigmoid(z[..., CONV_CH:])
    u_all = jnp.concatenate([past_u, u], axis=1)
    y = lax.conv_general_dilated(
        u_all, dw_w[:, None, :], window_strides=(1,), padding="VALID",
        dimension_numbers=("NWC", "WIO", "NWC"), feature_group_count=CONV_CH) + dw_b
    y = layernorm(y, ln_g, ln_b)
    y = y * jax.nn.sigmoid(y)
    return y, u_all[:, -(CONV_WIDTH - 1):]


def gla_chunk_step(S, xs):
    q, k, v, g = xs
    c = q.shape[2]
    b = jnp.cumsum(g, axis=2)
    q_t = q * jnp.exp(b)
    k_t = k * jnp.exp(-b)
    causal = jnp.tril(jnp.ones((c, c), dtype=bool))
    a = jnp.where(causal, jnp.einsum("bhtd,bhsd->bhts", q_t, k_t), 0.0)
    o = jnp.einsum("bhtd,bhde->bhte", q_t, S) + jnp.einsum("bhts,bhse->bhte", a, v)
    b_last = b[:, :, -1:, :]
    S = (jnp.exp(b_last)[:, :, 0, :, None] * S
         + jnp.einsum("bhsd,bhse->bhde", k * jnp.exp(b_last - b), v))
    return S, o


def gla_mixer(z, S0, w_a2, b_a, gn_g):
    B, L, _ = z.shape
    hk, hv = GLA_HEADS * GLA_DK, GLA_HEADS * GLA_DV
    q = z[..., :hk]
    k = z[..., hk:2 * hk]
    v = z[..., 2 * hk:2 * hk + hv]
    r = z[..., 2 * hk + hv:2 * hk + 2 * hv]
    a_lr = z[..., 2 * hk + 2 * hv:]
    log_a = jax.nn.log_sigmoid((a_lr @ w_a2 + b_a).astype(jnp.float32)) / GLA_TAU
    c = CHUNK if L % CHUNK == 0 else L
    nc = L // c

    def to_chunks(t, d):
        return t.astype(jnp.float32).reshape(B, nc, c, GLA_HEADS, d).transpose(1, 0, 3, 2, 4)

    xs = (to_chunks(q, GLA_DK) * GLA_DK ** -0.5, to_chunks(k, GLA_DK),
          to_chunks(v, GLA_DV), to_chunks(log_a, GLA_DK))
    S, o = lax.scan(gla_chunk_step, S0.astype(jnp.float32), xs)
    o = o.transpose(1, 0, 3, 2, 4).reshape(B, L, GLA_HEADS, GLA_DV)
    o = rmsnorm(o, gn_g).astype(z.dtype) * jax.nn.silu(r.reshape(B, L, GLA_HEADS, GLA_DV))
    return o.reshape(B, L, hv), S.astype(S0.dtype)


def trunk(x, p, past_ckv, past_krope, past_conv, past_gla, prm):
    B, L, _ = x.shape
    P = past_ckv.shape[2]
    pos = P + jnp.arange(L, dtype=jnp.int32)
    ckvs, krs, convs, glas = [], [], [], []
    for i in range(DEPTH):
        h = rmsnorm(x, prm["norm_mix"][i])
        z = h @ prm["w_in"][i]
        z_mla = z[..., :D_MLA_IN]
        z_conv = z[..., D_MLA_IN:D_MLA_IN + D_CONV_IN]
        z_gla = z[..., D_MLA_IN + D_CONV_IN:]
        o_a, ckv, kr = mla_mixer(z_mla, pos, past_ckv[i], past_krope[i], prm["mla_q_norm"][i],
                                 prm["mla_kv_norm"][i], prm["mla_w_uq"][i], prm["mla_w_uk"][i],
                                 prm["mla_w_uv"][i])
        o_b, conv_buf = conv_mixer(z_conv, past_conv[i], prm["conv_dw_w"][i], prm["conv_dw_b"][i],
                                   prm["conv_ln_g"][i], prm["conv_ln_b"][i])
        o_c, S = gla_mixer(z_gla, past_gla[i], prm["gla_w_a2"][i], prm["gla_b_a"][i],
                           prm["gla_norm"][i])
        x = x + jnp.concatenate([o_a, o_b, o_c], axis=-1) @ prm["w_out"][i]
        h = rmsnorm(x, prm["norm_ffn"][i])
        x = x + (jax.nn.silu(h @ prm["ffn_w_gate"][i]) * (h @ prm["ffn_w_up"][i])) @ prm["ffn_w_down"][i]
        h = rmsnorm(x, prm["norm_ple"][i])
        x = x + jax.nn.sigmoid(h @ prm["ple_w_gate"][i]) * (p[i] @ prm["ple_w_proj"][i])
        ckvs.append(ckv)
        krs.append(kr)
        convs.append(conv_buf)
        glas.append(S)
    y = rmsnorm(x, prm["norm_final"])
    return y, jnp.stack(ckvs), jnp.stack(krs), jnp.stack(convs), jnp.stack(glas)


def setup_inputs(seed: int = 0) -> dict:
    key = jax.random.key(seed)
    ks = jax.random.split(key, 32)
    f32 = jnp.float32

    def nrm(k, shape, scale):
        return jax.random.normal(k, shape, f32) * scale

    def gain(k, shape):
        return 1.0 + 0.05 * jax.random.normal(k, shape, f32)

    return {
        "x_prompt": nrm(ks[0], (BATCH, SEQ, D_MODEL), 1.0),
        "x_sample": nrm(ks[1], (DEC_BATCH, DEC_SEQ, D_MODEL), 1.0),
        "cache_ckv": nrm(ks[2], (DEPTH, DEC_BATCH, PAST_LEN, MLA_KV_RANK), 1.0),
        "cache_krope": nrm(ks[3], (DEPTH, DEC_BATCH, PAST_LEN, MLA_ROPE), 1.0),
        "cache_conv": nrm(ks[4], (DEPTH, DEC_BATCH, CONV_WIDTH - 1, CONV_CH), 0.5),
        "state_gla": nrm(ks[5], (DEPTH, DEC_BATCH, GLA_HEADS, GLA_DK, GLA_DV), 0.3),
        "p_prompt": nrm(ks[6], (DEPTH, BATCH, SEQ, D_PLE), 1.0),
        "p_sample": nrm(ks[7], (DEPTH, DEC_BATCH, DEC_SEQ, D_PLE), 1.0),
        "norm_mix": gain(ks[8], (DEPTH, D_MODEL)),
        "w_in": nrm(ks[9], (DEPTH, D_MODEL, D_IN), D_MODEL ** -0.5),
        "mla_q_norm": gain(ks[10], (DEPTH, MLA_Q_RANK)),
        "mla_kv_norm": gain(ks[11], (DEPTH, MLA_KV_RANK)),
        "mla_w_uq": nrm(ks[12], (DEPTH, MLA_Q_RANK, MLA_HEADS * (MLA_NOPE + MLA_ROPE)), MLA_Q_RANK ** -0.5),
        "mla_w_uk": nrm(ks[13], (DEPTH, MLA_KV_RANK, MLA_HEADS, MLA_NOPE), MLA_KV_RANK ** -0.5),
        "mla_w_uv": nrm(ks[14], (DEPTH, MLA_KV_RANK, MLA_HEADS, MLA_V), MLA_KV_RANK ** -0.5),
        "conv_dw_w": nrm(ks[15], (DEPTH, CONV_WIDTH, CONV_CH), CONV_WIDTH ** -0.5),
        "conv_dw_b": nrm(ks[16], (DEPTH, CONV_CH), 0.02),
        "conv_ln_g": gain(ks[17], (DEPTH, CONV_CH)),
        "conv_ln_b": nrm(ks[18], (DEPTH, CONV_CH), 0.02),
        "gla_w_a2": nrm(ks[19], (DEPTH, GLA_GATE_RANK, GLA_HEADS * GLA_DK), GLA_GATE_RANK ** -0.5),
        "gla_b_a": nrm(ks[20], (DEPTH, GLA_HEADS * GLA_DK), 0.02),
        "gla_norm": gain(ks[21], (DEPTH, GLA_DV)),
        "w_out": nrm(ks[22], (DEPTH, D_MIX, D_MODEL), D_MIX ** -0.5),
        "norm_ffn": gain(ks[23], (DEPTH, D_MODEL)),
        "ffn_w_gate": nrm(ks[24], (DEPTH, D_MODEL, D_FF), D_MODEL ** -0.5),
        "ffn_w_up": nrm(ks[25], (DEPTH, D_MODEL, D_FF), D_MODEL ** -0.5),
        "ffn_w_down": nrm(ks[26], (DEPTH, D_FF, D_MODEL), D_FF ** -0.5),
        "norm_ple": gain(ks[27], (DEPTH, D_MODEL)),
        "ple_w_gate": nrm(ks[28], (DEPTH, D_MODEL, D_MODEL), D_MODEL ** -0.5),
        "ple_w_proj": nrm(ks[29], (DEPTH, D_PLE, D_MODEL), D_PLE ** -0.5),
        "norm_final": gain(ks[30], (D_MODEL,)),
    }


def reference(x_prompt, x_sample, cache_ckv, cache_krope, cache_conv, state_gla, p_prompt, p_sample,
              norm_mix, w_in, mla_q_norm, mla_kv_norm, mla_w_uq, mla_w_uk, mla_w_uv,
              conv_dw_w, conv_dw_b, conv_ln_g, conv_ln_b, gla_w_a2, gla_b_a, gla_norm, w_out,
              norm_ffn, ffn_w_gate, ffn_w_up, ffn_w_down, norm_ple, ple_w_gate, ple_w_proj, norm_final):
    prm = {
        "norm_mix": norm_mix, "w_in": w_in, "mla_q_norm": mla_q_norm, "mla_kv_norm": mla_kv_norm,
        "mla_w_uq": mla_w_uq, "mla_w_uk": mla_w_uk, "mla_w_uv": mla_w_uv,
        "conv_dw_w": conv_dw_w, "conv_dw_b": conv_dw_b, "conv_ln_g": conv_ln_g, "conv_ln_b": conv_ln_b,
        "gla_w_a2": gla_w_a2, "gla_b_a": gla_b_a, "gla_norm": gla_norm, "w_out": w_out,
        "norm_ffn": norm_ffn, "ffn_w_gate": ffn_w_gate, "ffn_w_up": ffn_w_up, "ffn_w_down": ffn_w_down,
        "norm_ple": norm_ple, "ple_w_gate": ple_w_gate, "ple_w_proj": ple_w_proj, "norm_final": norm_final,
    }
    B = x_prompt.shape[0]
    dt = x_prompt.dtype
    empty_ckv = jnp.zeros((DEPTH, B, 0, MLA_KV_RANK), dt)
    empty_krope = jnp.zeros((DEPTH, B, 0, MLA_ROPE), dt)
    zero_conv = jnp.zeros((DEPTH, B, CONV_WIDTH - 1, CONV_CH), dt)
    zero_gla = jnp.zeros((DEPTH, B, GLA_HEADS, GLA_DK, GLA_DV), dt)
    y_prompt, ckv_p, kr_p, conv_p, gla_p = trunk(x_prompt, p_prompt, empty_ckv, empty_krope,
                                                 zero_conv, zero_gla, prm)
    y_sample, ckv_s, kr_s, conv_s, gla_s = trunk(x_sample, p_sample, cache_ckv, cache_krope,
                                                 cache_conv, state_gla, prm)
    return (y_prompt, y_sample, ckv_p, kr_p, conv_p, gla_p, ckv_s, kr_s, conv_s, gla_s)
```

```python
import functools

import jax
import jax.numpy as jnp
from jax import lax
from jax.experimental import pallas as pl
from jax.experimental.pallas import tpu as pltpu

F32 = jnp.float32
BF16 = jnp.bfloat16

D_MODEL = 4096
SEQ = 8192
DEPTH = 4
DEC_BATCH = 8
DEC_SEQ = 16
PAST_LEN = 1024
CHUNK = 64
EPS = 1e-6
D_PLE = 256
D_FF = 11008
MLA_HEADS = 16
MLA_Q_RANK = 1024
MLA_KV_RANK = 512
MLA_NOPE = 128
MLA_ROPE = 64
MLA_V = 128
MLA_QK = MLA_NOPE + MLA_ROPE
ROPE_THETA = 10000.0
CONV_CH = 1024
CONV_WIDTH = 31
CONV_HIST = CONV_WIDTH - 1
GLA_HEADS = 4
GLA_DK = 128
GLA_DV = 256
GLA_GATE_RANK = 16
GLA_TAU = 16.0
HK = GLA_HEADS * GLA_DK
HV = GLA_HEADS * GLA_DV

N_SAMPLE = DEC_BATCH * DEC_SEQ
ROWS = SEQ + N_SAMPLE
SAMPLE_KEYS = PAST_LEN + DEC_SEQ
SAMPLE_KEYS_PAD = 1152

LANES = 128
VMEM_LIMIT = 52 * 1024 * 1024

Z_CA = 0
Z_CB = 1024
Z_GV = 2048
Z_GR = 3072
Z_CQ = 4096
Z_GQ = 5120
Z_GK = 5632
Z_CKV = 6144
Z_KR = 6656
Z_AL = 6784
Z_W = 6912

TM = 832
TR = 416
TN_IN = 768
TN_OUT = 512
TN_FF = 256
TK_DOWN = D_FF // 2
TN_DOWN = 512
TN_PLE = 512
ATT_T = 256
CONV_T = 128
GLA_CPS = 4
GLA_CP = 64
XPOSE = 128


def _params(*sem):
    return pltpu.CompilerParams(dimension_semantics=sem, vmem_limit_bytes=VMEM_LIMIT)


def _rms(x, g):
    return x * lax.rsqrt(jnp.mean(x * x, axis=-1, keepdims=True) + EPS) * g


def _sigmoid(x):
    return 1.0 / (1.0 + jnp.exp(-x))


def _dot(a, b):
    return jnp.dot(a, b, preferred_element_type=F32)


def _dot_nt(a, b):
    return lax.dot_general(a, b, (((1,), (1,)), ((), ())), preferred_element_type=F32)


def _rmsnorm_kernel(x_ref, g_ref, o_ref):
    o_ref[...] = _rms(x_ref[...], g_ref[...]).astype(o_ref.dtype)


def rmsnorm_rows(x, g, out_dtype):
    rows, d = x.shape
    return pl.pallas_call(
        _rmsnorm_kernel,
        grid=(rows // TR,),
        in_specs=[pl.BlockSpec((TR, d), lambda i: (i, 0)),
                  pl.BlockSpec((1, d), lambda i: (0, 0))],
        out_specs=pl.BlockSpec((TR, d), lambda i: (i, 0)),
        out_shape=jax.ShapeDtypeStruct((rows, d), out_dtype),
        compiler_params=_params("parallel"),
        name="rmsnorm_rows",
    )(x, g.reshape(1, d))


def _matmul_kernel(a_ref, b_ref, o_ref):
    o_ref[...] = _dot(a_ref[...], b_ref[...]).astype(o_ref.dtype)


def in_proj(h, w):
    rows, k = h.shape
    n = w.shape[1]
    return pl.pallas_call(
        _matmul_kernel,
        grid=(rows // TM, n // TN_IN),
        in_specs=[pl.BlockSpec((TM, k), lambda i, j: (i, 0)),
                  pl.BlockSpec((k, TN_IN), lambda i, j: (0, j))],
        out_specs=pl.BlockSpec((TM, TN_IN), lambda i, j: (i, j)),
        out_shape=jax.ShapeDtypeStruct((rows, n), F32),
        compiler_params=_params("parallel", "arbitrary"),
        name="in_proj",
    )(h, w)


def _out_proj_kernel(oa_ref, ob_ref, oc_ref, wa_ref, wb_ref, wc_ref, x_ref, o_ref):
    acc = _dot(oa_ref[...], wa_ref[...])
    acc += _dot(ob_ref[...], wb_ref[...])
    acc += _dot(oc_ref[...], wc_ref[...])
    o_ref[...] = x_ref[...] + acc


def out_proj(o_a, o_b, o_c, w, x):
    rows = x.shape[0]
    da, db, dc = o_a.shape[1], o_b.shape[1], o_c.shape[1]
    return pl.pallas_call(
        _out_proj_kernel,
        grid=(rows // TM, D_MODEL // TN_OUT),
        in_specs=[pl.BlockSpec((TM, da), lambda i, j: (i, 0)),
                  pl.BlockSpec((TM, db), lambda i, j: (i, 0)),
                  pl.BlockSpec((TM, dc), lambda i, j: (i, 0)),
                  pl.BlockSpec((da, TN_OUT), lambda i, j: (0, j)),
                  pl.BlockSpec((db, TN_OUT), lambda i, j: (da // db, j)),
                  pl.BlockSpec((dc, TN_OUT), lambda i, j: ((da + db) // dc, j)),
                  pl.BlockSpec((TM, TN_OUT), lambda i, j: (i, j))],
        out_specs=pl.BlockSpec((TM, TN_OUT), lambda i, j: (i, j)),
        out_shape=jax.ShapeDtypeStruct((rows, D_MODEL), F32),
        compiler_params=_params("parallel", "arbitrary"),
        name="out_proj",
    )(o_a, o_b, o_c, w, w, w, x)


def _ffn_gate_up_kernel(h_ref, wg_ref, wu_ref, o_ref):
    h = h_ref[...]
    g = _dot(h, wg_ref[...])
    u = _dot(h, wu_ref[...])
    o_ref[...] = (g * _sigmoid(g) * u).astype(o_ref.dtype)


def ffn_gate_up(h, wg, wu):
    rows, k = h.shape
    return pl.pallas_call(
        _ffn_gate_up_kernel,
        grid=(rows // TM, D_FF // TN_FF),
        in_specs=[pl.BlockSpec((TM, k), lambda i, j: (i, 0)),
                  pl.BlockSpec((k, TN_FF), lambda i, j: (0, j)),
                  pl.BlockSpec((k, TN_FF), lambda i, j: (0, j))],
        out_specs=pl.BlockSpec((TM, TN_FF), lambda i, j: (i, j)),
        out_shape=jax.ShapeDtypeStruct((rows, D_FF), BF16),
        compiler_params=_params("parallel", "arbitrary"),
        name="ffn_gate_up",
    )(h, wg, wu)


def _ffn_down_kernel(a_ref, w_ref, x_ref, o_ref):
    part = _dot(a_ref[...], w_ref[...])

    @pl.when(pl.program_id(2) == 0)
    def _():
        o_ref[...] = x_ref[...] + part

    @pl.when(pl.program_id(2) != 0)
    def _():
        o_ref[...] += part


def ffn_down(a, w, x):
    rows = x.shape[0]
    return pl.pallas_call(
        _ffn_down_kernel,
        grid=(rows // TM, D_MODEL // TN_DOWN, D_FF // TK_DOWN),
        in_specs=[pl.BlockSpec((TM, TK_DOWN), lambda i, j, k: (i, k)),
                  pl.BlockSpec((TK_DOWN, TN_DOWN), lambda i, j, k: (k, j)),
                  pl.BlockSpec((TM, TN_DOWN), lambda i, j, k: (i, j))],
        out_specs=pl.BlockSpec((TM, TN_DOWN), lambda i, j, k: (i, j)),
        out_shape=jax.ShapeDtypeStruct((rows, D_MODEL), F32),
        compiler_params=_params("parallel", "arbitrary", "arbitrary"),
        name="ffn_down",
    )(a, w, x)


def _ple_kernel(h_ref, p_ref, wg_ref, wp_ref, x_ref, o_ref):
    gate = _sigmoid(_dot(h_ref[...], wg_ref[...]))
    proj = _dot(p_ref[...].astype(BF16), wp_ref[...])
    o_ref[...] = x_ref[...] + gate * proj


def ple_update(h, p, wg, wp, x):
    rows, k = h.shape
    return pl.pallas_call(
        _ple_kernel,
        grid=(rows // TM, D_MODEL // TN_PLE),
        in_specs=[pl.BlockSpec((TM, k), lambda i, j: (i, 0)),
                  pl.BlockSpec((TM, D_PLE), lambda i, j: (i, 0)),
                  pl.BlockSpec((k, TN_PLE), lambda i, j: (0, j)),
                  pl.BlockSpec((D_PLE, TN_PLE), lambda i, j: (0, j)),
                  pl.BlockSpec((TM, TN_PLE), lambda i, j: (i, j))],
        out_specs=pl.BlockSpec((TM, TN_PLE), lambda i, j: (i, j)),
        out_shape=jax.ShapeDtypeStruct((rows, D_MODEL), F32),
        compiler_params=_params("parallel", "arbitrary"),
        name="ple_update",
    )(h, p, wg, wp, x)


def _swap_rope_halves(x):
    lane = lax.broadcasted_iota(jnp.int32, x.shape, 1)
    first = (lane % MLA_ROPE) < (MLA_ROPE // 2)
    return jnp.where(first, pltpu.roll(x, LANES - MLA_ROPE // 2, 1), pltpu.roll(x, MLA_ROPE // 2, 1))


def _mla_q_kernel(z_ref, g_ref, w_ref, c1_ref, s1_ref, c2_ref, s2_ref, q_ref):
    cq = _rms(z_ref[...], g_ref[...]).astype(BF16)
    q = _dot(cq, w_ref[...])
    scale = MLA_QK ** -0.5
    for pair in range(MLA_HEADS // 2):
        base = pair * 3 * LANES
        col0 = q[:, base:base + LANES]
        col1 = q[:, base + LANES:base + 2 * LANES]
        col2 = q[:, base + 2 * LANES:base + 3 * LANES]
        col1 = col1 * c1_ref[...] + _swap_rope_halves(col1) * s1_ref[...]
        col2 = col2 * c2_ref[...] + _swap_rope_halves(col2) * s2_ref[...]
        both = jnp.concatenate([col0, col1, col2], axis=1) * scale
        q_ref[2 * pair] = both[:, :MLA_QK].astype(BF16)
        q_ref[2 * pair + 1] = both[:, MLA_QK:].astype(BF16)


def mla_q(z, g, w_uq, tabs):
    rows = z.shape[0]
    tab_spec = pl.BlockSpec((TR, LANES), lambda i: (i, 0))
    return pl.pallas_call(
        _mla_q_kernel,
        grid=(rows // TR,),
        in_specs=[pl.BlockSpec((TR, MLA_Q_RANK), lambda i: (i, Z_CQ // MLA_Q_RANK)),
                  pl.BlockSpec((1, MLA_Q_RANK), lambda i: (0, 0)),
                  pl.BlockSpec(w_uq.shape, lambda i: (0, 0)),
                  tab_spec, tab_spec, tab_spec, tab_spec],
        out_specs=pl.BlockSpec((MLA_HEADS, TR, MLA_QK), lambda i: (0, i, 0)),
        out_shape=jax.ShapeDtypeStruct((MLA_HEADS, rows, MLA_QK), BF16),
        compiler_params=_params("parallel"),
        name="mla_q",
    )(z, g.reshape(1, -1), w_uq, *tabs)


def _mla_latent_kernel(zc_ref, zk_ref, g_ref, ck_ref, sk_ref, ckv_ref, kr_ref):
    ckv_ref[...] = _rms(zc_ref[...], g_ref[...])
    kr = zk_ref[...]
    rot = kr * ck_ref[...] + _swap_rope_halves(kr) * sk_ref[...]
    kr_ref[...] = rot[:, :MLA_ROPE]


def mla_latent(z, g, ck, sk):
    rows = z.shape[0]
    return pl.pallas_call(
        _mla_latent_kernel,
        grid=(rows // TR,),
        in_specs=[pl.BlockSpec((TR, MLA_KV_RANK), lambda i: (i, Z_CKV // MLA_KV_RANK)),
                  pl.BlockSpec((TR, LANES), lambda i: (i, Z_KR // LANES)),
                  pl.BlockSpec((1, MLA_KV_RANK), lambda i: (0, 0)),
                  pl.BlockSpec((TR, LANES), lambda i: (i, 0)),
                  pl.BlockSpec((TR, LANES), lambda i: (i, 0))],
        out_specs=[pl.BlockSpec((TR, MLA_KV_RANK), lambda i: (i, 0)),
                   pl.BlockSpec((TR, MLA_ROPE), lambda i: (i, 0))],
        out_shape=[jax.ShapeDtypeStruct((rows, MLA_KV_RANK), F32),
                   jax.ShapeDtypeStruct((rows, MLA_ROPE), F32)],
        compiler_params=_params("parallel"),
        name="mla_latent",
    )(z, z, g.reshape(1, -1), ck, sk)


def _kv_proj_kernel(ckv_ref, kr_ref, wk_ref, wv_ref, k_ref, v_ref):
    c = ckv_ref[...].astype(BF16)
    kn = _dot(c, wk_ref[...])
    v = _dot(c, wv_ref[...])
    kr = kr_ref[...].astype(BF16)
    for h in range(MLA_HEADS):
        k_ref[h, :, :MLA_NOPE] = kn[:, h * MLA_NOPE:(h + 1) * MLA_NOPE].astype(BF16)
        k_ref[h, :, MLA_NOPE:] = kr
        v_ref[h] = v[:, h * MLA_V:(h + 1) * MLA_V].astype(BF16)


def kv_proj(ckv, kr, w_uk, w_uv):
    rows = ckv.shape[0]
    tile = TR if rows % TR == 0 else 512
    return pl.pallas_call(
        _kv_proj_kernel,
        grid=(rows // tile,),
        in_specs=[pl.BlockSpec((tile, MLA_KV_RANK), lambda i: (i, 0)),
                  pl.BlockSpec((tile, MLA_ROPE), lambda i: (i, 0)),
                  pl.BlockSpec(w_uk.shape, lambda i: (0, 0)),
                  pl.BlockSpec(w_uv.shape, lambda i: (0, 0))],
        out_specs=[pl.BlockSpec((MLA_HEADS, tile, MLA_QK), lambda i: (0, i, 0)),
                   pl.BlockSpec((MLA_HEADS, tile, MLA_V), lambda i: (0, i, 0))],
        out_shape=[jax.ShapeDtypeStruct((MLA_HEADS, rows, MLA_QK), BF16),
                   jax.ShapeDtypeStruct((MLA_HEADS, rows, MLA_V), BF16)],
        compiler_params=_params("parallel"),
        name="kv_proj",
    )(ckv, kr, w_uk, w_uv)


def _attn_prompt_kernel(q_ref, k_ref, v_ref, o_ref, m_sc, l_sc, acc_sc):
    i = pl.program_id(1)
    q = q_ref[0]
    m_sc[...] = jnp.full(m_sc.shape, -jnp.inf, F32)
    l_sc[...] = jnp.zeros(l_sc.shape, F32)
    acc_sc[...] = jnp.zeros(acc_sc.shape, F32)

    def step(j, masked):
        start = pl.multiple_of(j * ATT_T, ATT_T)
        k = k_ref[0, pl.ds(start, ATT_T), :]
        v = v_ref[0, pl.ds(start, ATT_T), :]
        s = _dot_nt(q, k)
        if masked:
            qc = lax.broadcasted_iota(jnp.int32, s.shape, 0) // CHUNK
            kc = lax.broadcasted_iota(jnp.int32, s.shape, 1) // CHUNK
            s = jnp.where(kc <= qc, s, -jnp.inf)
        m_prev = m_sc[...]
        m_new = jnp.maximum(m_prev, jnp.max(s, axis=-1, keepdims=True))
        alpha = jnp.exp(m_prev - m_new)
        p = jnp.exp(s - m_new)
        l_sc[...] = alpha * l_sc[...] + jnp.sum(p, axis=-1, keepdims=True)
        acc_sc[...] = alpha * acc_sc[...] + _dot(p.astype(BF16), v)
        m_sc[...] = m_new

    def body(j, carry):
        step(j, False)
        return carry

    lax.fori_loop(0, i, body, 0)
    step(i, True)
    o_ref[...] = (acc_sc[...] / l_sc[...]).astype(o_ref.dtype)


def attn_prompt(q, k, v):
    return pl.pallas_call(
        _attn_prompt_kernel,
        grid=(MLA_HEADS, SEQ // ATT_T),
        in_specs=[pl.BlockSpec((1, ATT_T, MLA_QK), lambda h, i: (h, i, 0)),
                  pl.BlockSpec((1, SEQ, MLA_QK), lambda h, i: (h, 0, 0)),
                  pl.BlockSpec((1, SEQ, MLA_V), lambda h, i: (h, 0, 0))],
        out_specs=pl.BlockSpec((ATT_T, MLA_V), lambda h, i: (i, h)),
        out_shape=jax.ShapeDtypeStruct((SEQ, MLA_HEADS * MLA_V), BF16),
        scratch_shapes=[pltpu.VMEM((ATT_T, 1), F32), pltpu.VMEM((ATT_T, 1), F32),
                        pltpu.VMEM((ATT_T, MLA_V), F32)],
        compiler_params=_params("parallel", "arbitrary"),
        name="attn_prompt",
    )(q, k, v)


def _attn_sample_kernel(q_ref, k_ref, v_ref, o_ref):
    s = _dot_nt(q_ref[0], k_ref[0])
    qc = (PAST_LEN + lax.broadcasted_iota(jnp.int32, s.shape, 0)) // CHUNK
    kpos = lax.broadcasted_iota(jnp.int32, s.shape, 1)
    s = jnp.where(kpos // CHUNK <= qc, s, -jnp.inf)
    s = jnp.where(kpos < SAMPLE_KEYS, s, -jnp.inf)
    m = jnp.max(s, axis=-1, keepdims=True)
    p = jnp.exp(s - m)
    w = p / jnp.sum(p, axis=-1, keepdims=True)
    o_ref[...] = _dot(w.astype(BF16), v_ref[0]).astype(o_ref.dtype)


def attn_sample(q, k, v):
    return pl.pallas_call(
        _attn_sample_kernel,
        grid=(DEC_BATCH, MLA_HEADS),
        in_specs=[pl.BlockSpec((1, DEC_SEQ, MLA_QK), lambda b, h: (h, SEQ // DEC_SEQ + b, 0)),
                  pl.BlockSpec((1, SAMPLE_KEYS_PAD, MLA_QK), lambda b, h: (h, b, 0)),
                  pl.BlockSpec((1, SAMPLE_KEYS_PAD, MLA_V), lambda b, h: (h, b, 0))],
        out_specs=pl.BlockSpec((DEC_SEQ, MLA_V), lambda b, h: (b, h)),
        out_shape=jax.ShapeDtypeStruct((N_SAMPLE, MLA_HEADS * MLA_V), BF16),
        compiler_params=_params("parallel", "arbitrary"),
        name="attn_sample",
    )(q, k, v)


CONV_PAD = 32
CONV_OFF = CONV_PAD - CONV_HIST


def _conv_kernel(za_ref, zb_ref, past_ref, w_ref, b_ref, g_ref, beta_ref, o_ref, hist_ref, ext_sc, *, t_rows):
    @pl.when(pl.program_id(1) == 0)
    def _():
        ext_sc[CONV_OFF:CONV_PAD, :] = past_ref[0]

    u = za_ref[...] * _sigmoid(zb_ref[...])
    ext_sc[CONV_PAD:CONV_PAD + t_rows, :] = u
    cols = []
    for c0 in range(0, CONV_CH, LANES):
        acc = jnp.zeros((t_rows, LANES), F32)
        for tap in range(CONV_WIDTH):
            acc += w_ref[tap:tap + 1, c0:c0 + LANES] * ext_sc[CONV_OFF + tap:CONV_OFF + tap + t_rows, c0:c0 + LANES]
        cols.append(acc)
    y = jnp.concatenate(cols, axis=1) + b_ref[...]
    yc = y - jnp.mean(y, axis=-1, keepdims=True)
    y = yc * lax.rsqrt(jnp.mean(yc * yc, axis=-1, keepdims=True) + EPS) * g_ref[...] + beta_ref[...]
    o_ref[...] = (y * _sigmoid(y)).astype(o_ref.dtype)
    tail = ext_sc[t_rows + CONV_OFF:t_rows + CONV_PAD, :]
    hist_ref[0] = tail
    ext_sc[CONV_OFF:CONV_PAD, :] = tail


def conv_mixer(z, past, w, b, g, beta, *, batch, seq, row_off, t_rows):
    steps = seq // t_rows
    first = row_off // t_rows

    def rows(bi, t):
        return first + bi * steps + t

    vec = pl.BlockSpec((1, CONV_CH), lambda bi, t: (0, 0))
    return pl.pallas_call(
        functools.partial(_conv_kernel, t_rows=t_rows),
        grid=(batch, steps),
        in_specs=[pl.BlockSpec((t_rows, CONV_CH), lambda bi, t: (rows(bi, t), Z_CA // CONV_CH)),
                  pl.BlockSpec((t_rows, CONV_CH), lambda bi, t: (rows(bi, t), Z_CB // CONV_CH)),
                  pl.BlockSpec((1, CONV_HIST, CONV_CH), lambda bi, t: (bi, 0, 0)),
                  pl.BlockSpec((CONV_WIDTH, CONV_CH), lambda bi, t: (0, 0)),
                  vec, vec, vec],
        out_specs=[pl.BlockSpec((t_rows, CONV_CH), lambda bi, t: (bi * steps + t, 0)),
                   pl.BlockSpec((1, CONV_HIST, CONV_CH), lambda bi, t: (bi, 0, 0))],
        out_shape=[jax.ShapeDtypeStruct((batch * seq, CONV_CH), BF16),
                   jax.ShapeDtypeStruct((batch, CONV_HIST, CONV_CH), F32)],
        scratch_shapes=[pltpu.VMEM((CONV_PAD + t_rows, CONV_CH), F32)],
        compiler_params=_params("parallel", "arbitrary"),
        name="conv_mixer",
    )(z, z, past, w, b.reshape(1, -1), g.reshape(1, -1), beta.reshape(1, -1))


def _pad_rows(x, rows):
    if x.shape[0] == rows:
        return x
    return jnp.concatenate([x, jnp.zeros((rows - x.shape[0], x.shape[1]), x.dtype)], axis=0)


def _log_sigmoid(x):
    return jnp.minimum(x, 0.0) - jnp.log1p(jnp.exp(-jnp.abs(x)))


def _gla_kernel(q_ref, k_ref, v_ref, r_ref, al_ref, wa_ref, ba_ref, gn_ref, s0_ref, o_ref, s_out_ref, s_sc,
                *, c, cps):
    step = pl.program_id(1)

    @pl.when(step == 0)
    def _():
        s_sc[...] = s0_ref[0]

    row = lax.broadcasted_iota(jnp.int32, (GLA_CP, GLA_CP), 0)
    col = lax.broadcasted_iota(jnp.int32, (GLA_CP, GLA_CP), 1)
    causal = col <= row
    tri = causal.astype(F32)
    for ci in range(cps):
        r0 = ci * c
        gate = _dot(al_ref[r0:r0 + c, :].astype(BF16), wa_ref[...]) + ba_ref[...]
        log_a = _pad_rows(_log_sigmoid(gate) / GLA_TAU, GLA_CP)
        bcum = jnp.dot(tri, log_a, precision=lax.Precision.HIGHEST, preferred_element_type=F32)
        for h in range(GLA_HEADS):
            ks = slice(h * GLA_DK, (h + 1) * GLA_DK)
            vs = slice(h * GLA_DV, (h + 1) * GLA_DV)
            b = bcum[:, ks]
            q = _pad_rows(q_ref[r0:r0 + c, ks], GLA_CP) * GLA_DK ** -0.5
            k = _pad_rows(k_ref[r0:r0 + c, ks], GLA_CP)
            v = _pad_rows(v_ref[r0:r0 + c, vs], GLA_CP).astype(BF16)
            q_t = (q * jnp.exp(b)).astype(BF16)
            k_t = (k * jnp.exp(-b)).astype(BF16)
            a = jnp.where(causal, _dot_nt(q_t, k_t), 0.0)
            s_prev = s_sc[h]
            o = _dot(q_t, s_prev.astype(BF16)) + _dot(a.astype(BF16), v)
            b_last = b[c - 1:c, :]
            k_dec = k * jnp.exp(b_last - b)
            k_dec_t = _pad_rows(k_dec, XPOSE).T
            decay = jnp.exp(_pad_rows(b, XPOSE).T[:, c - 1:c])
            s_sc[h] = decay * s_prev + _dot(k_dec_t.astype(BF16), _pad_rows(v, XPOSE))
            on = _rms(o[:c], gn_ref[...])
            r = r_ref[r0:r0 + c, vs]
            o_ref[r0:r0 + c, vs] = (on * (r * _sigmoid(r))).astype(o_ref.dtype)

    @pl.when(step == pl.num_programs(1) - 1)
    def _():
        s_out_ref[0] = s_sc[...]


def gla_mixer(z, s0, wa2, ba, gn, *, batch, seq, row_off, c, cps):
    t_rows = c * cps
    steps = seq // t_rows
    first = row_off // t_rows

    def rows(bi, t):
        return first + bi * steps + t

    state = pl.BlockSpec((1, GLA_HEADS, GLA_DK, GLA_DV), lambda bi, t: (bi, 0, 0, 0))
    return pl.pallas_call(
        functools.partial(_gla_kernel, c=c, cps=cps),
        grid=(batch, steps),
        in_specs=[pl.BlockSpec((t_rows, HK), lambda bi, t: (rows(bi, t), Z_GQ // HK)),
                  pl.BlockSpec((t_rows, HK), lambda bi, t: (rows(bi, t), Z_GK // HK)),
                  pl.BlockSpec((t_rows, HV), lambda bi, t: (rows(bi, t), Z_GV // HV)),
                  pl.BlockSpec((t_rows, HV), lambda bi, t: (rows(bi, t), Z_GR // HV)),
                  pl.BlockSpec((t_rows, LANES), lambda bi, t: (rows(bi, t), Z_AL // LANES)),
                  pl.BlockSpec((LANES, HK), lambda bi, t: (0, 0)),
                  pl.BlockSpec((1, HK), lambda bi, t: (0, 0)),
                  pl.BlockSpec((1, GLA_DV), lambda bi, t: (0, 0)),
                  state],
        out_specs=[pl.BlockSpec((t_rows, HV), lambda bi, t: (bi * steps + t, 0)), state],
        out_shape=[jax.ShapeDtypeStruct((batch * seq, HV), BF16),
                   jax.ShapeDtypeStruct((batch, GLA_HEADS, GLA_DK, GLA_DV), F32)],
        scratch_shapes=[pltpu.VMEM((GLA_HEADS, GLA_DK, GLA_DV), F32)],
        compiler_params=_params("parallel", "arbitrary"),
        name="gla_mixer",
    )(z, z, z, z, z, wa2, ba.reshape(1, -1), gn.reshape(1, -1), s0)


def _pack_w_in(w_in):
    w = w_in.astype(BF16)
    mla, conv = 0, MLA_Q_RANK + MLA_KV_RANK + MLA_ROPE
    gla = conv + 2 * CONV_CH

    def cols(a, n):
        return w[..., a:a + n]

    def zeros(n):
        return jnp.zeros(w.shape[:-1] + (n,), BF16)

    return jnp.concatenate([
        cols(conv, CONV_CH), cols(conv + CONV_CH, CONV_CH),
        cols(gla + 2 * HK, HV), cols(gla + 2 * HK + HV, HV),
        cols(mla, MLA_Q_RANK),
        cols(gla, HK), cols(gla + HK, HK),
        cols(mla + MLA_Q_RANK, MLA_KV_RANK),
        cols(mla + MLA_Q_RANK + MLA_KV_RANK, MLA_ROPE), zeros(LANES - MLA_ROPE),
        cols(gla + 2 * HK + 2 * HV, GLA_GATE_RANK), zeros(LANES - GLA_GATE_RANK),
    ], axis=-1)


def _rope_tables():
    half = MLA_ROPE // 2
    pos = jnp.concatenate([jnp.arange(SEQ, dtype=jnp.int32),
                           jnp.tile(PAST_LEN + jnp.arange(DEC_SEQ, dtype=jnp.int32), DEC_BATCH)])
    inv_freq = ROPE_THETA ** (-jnp.arange(half, dtype=F32) / half)
    ang = pos.astype(F32)[:, None] * inv_freq[None, :]
    cos, sin = jnp.cos(ang), jnp.sin(ang)
    c64 = jnp.concatenate([cos, cos], axis=1)
    s64 = jnp.concatenate([-sin, sin], axis=1)
    one, zero = jnp.ones_like(c64), jnp.zeros_like(s64)
    lo = (jnp.concatenate([c64, one], axis=1), jnp.concatenate([s64, zero], axis=1))
    hi = (jnp.concatenate([one, c64], axis=1), jnp.concatenate([zero, s64], axis=1))
    key = (jnp.concatenate([c64, zero], axis=1), jnp.concatenate([s64, zero], axis=1))
    return (lo[0], lo[1], hi[0], hi[1]), key


@jax.jit
def _forward(x_prompt, x_sample, cache_ckv, cache_krope, cache_conv, state_gla, p_prompt, p_sample, norm_mix,
             w_in, mla_q_norm, mla_kv_norm, mla_w_uq, mla_w_uk, mla_w_uv, conv_dw_w, conv_dw_b, conv_ln_g,
             conv_ln_b, gla_w_a2, gla_b_a, gla_norm, w_out, norm_ffn, ffn_w_gate, ffn_w_up, ffn_w_down,
             norm_ple, ple_w_gate, ple_w_proj, norm_final):
    w_in_p = _pack_w_in(w_in)
    w_uq = mla_w_uq.astype(BF16)
    w_uk = mla_w_uk.reshape(DEPTH, MLA_KV_RANK, MLA_HEADS * MLA_NOPE).astype(BF16)
    w_uv = mla_w_uv.reshape(DEPTH, MLA_KV_RANK, MLA_HEADS * MLA_V).astype(BF16)
    w_a2 = jnp.concatenate([gla_w_a2.astype(BF16),
                            jnp.zeros((DEPTH, LANES - GLA_GATE_RANK, HK), BF16)], axis=1)
    w_out_b = w_out.astype(BF16)
    w_gate, w_up, w_down = ffn_w_gate.astype(BF16), ffn_w_up.astype(BF16), ffn_w_down.astype(BF16)
    w_pg, w_pp = ple_w_gate.astype(BF16), ple_w_proj.astype(BF16)
    q_tabs, (ck, sk) = _rope_tables()

    x = jnp.concatenate([x_prompt.reshape(SEQ, D_MODEL), x_sample.reshape(N_SAMPLE, D_MODEL)], axis=0)
    p_all = jnp.concatenate([p_prompt.reshape(DEPTH, SEQ, D_PLE), p_sample.reshape(DEPTH, N_SAMPLE, D_PLE)], axis=1)
    zero_conv = jnp.zeros((1, CONV_HIST, CONV_CH), F32)
    zero_gla = jnp.zeros((1, GLA_HEADS, GLA_DK, GLA_DV), F32)

    ckvs, krs, convs_p, convs_s, glas_p, glas_s = [], [], [], [], [], []
    for i in range(DEPTH):
        h = rmsnorm_rows(x, norm_mix[i], BF16)
        z = in_proj(h, w_in_p[i])

        q = mla_q(z, mla_q_norm[i], w_uq[i], q_tabs)
        ckv, kr = mla_latent(z, mla_kv_norm[i], ck, sk)
        k_p, v_p = kv_proj(ckv[:SEQ], kr[:SEQ], w_uk[i], w_uv[i])
        pad = SAMPLE_KEYS_PAD - SAMPLE_KEYS
        ckv_s = jnp.concatenate([cache_ckv[i], ckv[SEQ:].reshape(DEC_BATCH, DEC_SEQ, MLA_KV_RANK),
                                 jnp.zeros((DEC_BATCH, pad, MLA_KV_RANK), F32)], axis=1)
        kr_s = jnp.concatenate([cache_krope[i], kr[SEQ:].reshape(DEC_BATCH, DEC_SEQ, MLA_ROPE),
                                jnp.zeros((DEC_BATCH, pad, MLA_ROPE), F32)], axis=1)
        k_s, v_s = kv_proj(ckv_s.reshape(-1, MLA_KV_RANK), kr_s.reshape(-1, MLA_ROPE), w_uk[i], w_uv[i])
        o_a = jnp.concatenate([attn_prompt(q, k_p, v_p), attn_sample(q, k_s, v_s)], axis=0)

        conv_args = (conv_dw_w[i], conv_dw_b[i], conv_ln_g[i], conv_ln_b[i])
        ob_p, conv_p = conv_mixer(z, zero_conv, *conv_args, batch=1, seq=SEQ, row_off=0, t_rows=CONV_T)
        ob_s, conv_s = conv_mixer(z, cache_conv[i], *conv_args, batch=DEC_BATCH, seq=DEC_SEQ, row_off=SEQ,
                                  t_rows=DEC_SEQ)
        o_b = jnp.concatenate([ob_p, ob_s], axis=0)

        gla_args = (w_a2[i], gla_b_a[i], gla_norm[i])
        oc_p, gla_p = gla_mixer(z, zero_gla, *gla_args, batch=1, seq=SEQ, row_off=0, c=CHUNK, cps=GLA_CPS)
        oc_s, gla_s = gla_mixer(z, state_gla[i], *gla_args, batch=DEC_BATCH, seq=DEC_SEQ, row_off=SEQ,
                                c=DEC_SEQ, cps=1)
        o_c = jnp.concatenate([oc_p, oc_s], axis=0)

        x = out_proj(o_a, o_b, o_c, w_out_b[i], x)
        h = rmsnorm_rows(x, norm_ffn[i], BF16)
        x = ffn_down(ffn_gate_up(h, w_gate[i], w_up[i]), w_down[i], x)
        h = rmsnorm_rows(x, norm_ple[i], BF16)
        x = ple_update(h, p_all[i], w_pg[i], w_pp[i], x)

        ckvs.append(ckv)
        krs.append(kr)
        convs_p.append(conv_p)
        convs_s.append(conv_s)
        glas_p.append(gla_p)
        glas_s.append(gla_s)

    y = rmsnorm_rows(x, norm_final, F32)
    ckv_all, kr_all = jnp.stack(ckvs), jnp.stack(krs)
    return (y[:SEQ].reshape(1, SEQ, D_MODEL),
            y[SEQ:].reshape(DEC_BATCH, DEC_SEQ, D_MODEL),
            ckv_all[:, :SEQ].reshape(DEPTH, 1, SEQ, MLA_KV_RANK),
            kr_all[:, :SEQ].reshape(DEPTH, 1, SEQ, MLA_ROPE),
            jnp.stack(convs_p),
            jnp.stack(glas_p),
            ckv_all[:, SEQ:].reshape(DEPTH, DEC_BATCH, DEC_SEQ, MLA_KV_RANK),
            kr_all[:, SEQ:].reshape(DEPTH, DEC_BATCH, DEC_SEQ, MLA_ROPE),
            jnp.stack(convs_s),
            jnp.stack(glas_s))


def kernel(x_prompt, x_sample, cache_ckv, cache_krope, cache_conv, state_gla, p_prompt, p_sample, norm_mix, w_in, mla_q_norm, mla_kv_norm, mla_w_uq, mla_w_uk, mla_w_uv, conv_dw_w, conv_dw_b, conv_ln_g, conv_ln_b, gla_w_a2, gla_b_a, gla_norm, w_out, norm_ffn, ffn_w_gate, ffn_w_up, ffn_w_down, norm_ple, ple_w_gate, ple_w_proj, norm_final):
    return _forward(x_prompt, x_sample, cache_ckv, cache_krope, cache_conv, state_gla, p_prompt, p_sample,
                    norm_mix, w_in, mla_q_norm, mla_kv_norm, mla_w_uq, mla_w_uk, mla_w_uv, conv_dw_w, conv_dw_b,
                    conv_ln_g, conv_ln_b, gla_w_a2, gla_b_a, gla_norm, w_out, norm_ffn, ffn_w_gate, ffn_w_up,
                    ffn_w_down, norm_ple, ple_w_gate, ple_w_proj, norm_final)
```

```python
import functools

import jax
import jax.numpy as jnp
from jax import lax
from jax.experimental import pallas as pl
from jax.experimental.pallas import tpu as pltpu

F32 = jnp.float32
BF16 = jnp.bfloat16

D_MODEL = 4096
SEQ = 8192
DEPTH = 4
DEC_BATCH = 8
DEC_SEQ = 16
PAST_LEN = 1024
CHUNK = 64
EPS = 1e-6
D_PLE = 256
D_FF = 11008
MLA_HEADS = 16
MLA_Q_RANK = 1024
MLA_KV_RANK = 512
MLA_NOPE = 128
MLA_ROPE = 64
MLA_V = 128
MLA_QK = MLA_NOPE + MLA_ROPE
ROPE_THETA = 10000.0
CONV_CH = 1024
CONV_WIDTH = 31
CONV_HIST = CONV_WIDTH - 1
GLA_HEADS = 4
GLA_DK = 128
GLA_DV = 256
GLA_GATE_RANK = 16
GLA_TAU = 16.0
HK = GLA_HEADS * GLA_DK
HV = GLA_HEADS * GLA_DV

N_SAMPLE = DEC_BATCH * DEC_SEQ
ROWS = SEQ + N_SAMPLE
SAMPLE_KEYS = PAST_LEN + DEC_SEQ
SAMPLE_KEYS_PAD = 1152

LANES = 128
VMEM_LIMIT = 52 * 1024 * 1024

Z_CA = 0
Z_CB = 1024
Z_GV = 2048
Z_GR = 3072
Z_CQ = 4096
Z_GQ = 5120
Z_GK = 5632
Z_CKV = 6144
Z_KR = 6656
Z_AL = 6784
Z_W = 6912

TM = 832
TR = 416
TN_IN = 768
TN_OUT = 512
TN_FF = 256
TK_DOWN = D_FF // 2
TN_DOWN = 512
TN_PLE = 512
TQ = 640
ATT_T = 512
ATT_HB = 2
Q_SCALE = MLA_QK ** -0.5 * 1.4426950408889634
CONV_T = 128
GLA_CPS = 4
GLA_CP = 64
XPOSE = 128


def _params(*sem):
    return pltpu.CompilerParams(dimension_semantics=sem, vmem_limit_bytes=VMEM_LIMIT)


def _rms(x, g):
    return x * lax.rsqrt(jnp.mean(x * x, axis=-1, keepdims=True) + EPS) * g


def _sigmoid(x):
    return 1.0 / (1.0 + jnp.exp(-x))


def _dot(a, b):
    return jnp.dot(a, b, preferred_element_type=F32)


def _dot_nt(a, b):
    return lax.dot_general(a, b, (((1,), (1,)), ((), ())), preferred_element_type=F32)


def _rmsnorm_kernel(x_ref, g_ref, o_ref):
    o_ref[...] = _rms(x_ref[...], g_ref[...]).astype(o_ref.dtype)


def rmsnorm_rows(x, g, out_dtype):
    rows, d = x.shape
    return pl.pallas_call(
        _rmsnorm_kernel,
        grid=(rows // TR,),
        in_specs=[pl.BlockSpec((TR, d), lambda i: (i, 0)),
                  pl.BlockSpec((1, d), lambda i: (0, 0))],
        out_specs=pl.BlockSpec((TR, d), lambda i: (i, 0)),
        out_shape=jax.ShapeDtypeStruct((rows, d), out_dtype),
        compiler_params=_params("parallel"),
        name="rmsnorm_rows",
    )(x, g.reshape(1, d))


def _matmul_kernel(a_ref, b_ref, o_ref):
    o_ref[...] = _dot(a_ref[...], b_ref[...]).astype(o_ref.dtype)


def in_proj(h, w):
    rows, k = h.shape
    n = w.shape[1]
    return pl.pallas_call(
        _matmul_kernel,
        grid=(rows // TM, n // TN_IN),
        in_specs=[pl.BlockSpec((TM, k), lambda i, j: (i, 0)),
                  pl.BlockSpec((k, TN_IN), lambda i, j: (0, j))],
        out_specs=pl.BlockSpec((TM, TN_IN), lambda i, j: (i, j)),
        out_shape=jax.ShapeDtypeStruct((rows, n), F32),
        compiler_params=_params("parallel", "arbitrary"),
        name="in_proj",
    )(h, w)


def _out_proj_kernel(oa_ref, ob_ref, oc_ref, wa_ref, wb_ref, wc_ref, x_ref, o_ref):
    acc = _dot(oa_ref[...], wa_ref[...])
    acc += _dot(ob_ref[...], wb_ref[...])
    acc += _dot(oc_ref[...], wc_ref[...])
    o_ref[...] = x_ref[...] + acc


def out_proj(o_a, o_b, o_c, w, x):
    rows = x.shape[0]
    da, db, dc = o_a.shape[1], o_b.shape[1], o_c.shape[1]
    return pl.pallas_call(
        _out_proj_kernel,
        grid=(rows // TM, D_MODEL // TN_OUT),
        in_specs=[pl.BlockSpec((TM, da), lambda i, j: (i, 0)),
                  pl.BlockSpec((TM, db), lambda i, j: (i, 0)),
                  pl.BlockSpec((TM, dc), lambda i, j: (i, 0)),
                  pl.BlockSpec((da, TN_OUT), lambda i, j: (0, j)),
                  pl.BlockSpec((db, TN_OUT), lambda i, j: (da // db, j)),
                  pl.BlockSpec((dc, TN_OUT), lambda i, j: ((da + db) // dc, j)),
                  pl.BlockSpec((TM, TN_OUT), lambda i, j: (i, j))],
        out_specs=pl.BlockSpec((TM, TN_OUT), lambda i, j: (i, j)),
        out_shape=jax.ShapeDtypeStruct((rows, D_MODEL), F32),
        compiler_params=_params("parallel", "arbitrary"),
        name="out_proj",
    )(o_a, o_b, o_c, w, w, w, x)


def _ffn_gate_up_kernel(h_ref, wg_ref, wu_ref, o_ref):
    h = h_ref[...]
    g = _dot(h, wg_ref[...])
    u = _dot(h, wu_ref[...])
    o_ref[...] = (g * _sigmoid(g) * u).astype(o_ref.dtype)


def ffn_gate_up(h, wg, wu):
    rows, k = h.shape
    return pl.pallas_call(
        _ffn_gate_up_kernel,
        grid=(rows // TM, D_FF // TN_FF),
        in_specs=[pl.BlockSpec((TM, k), lambda i, j: (i, 0)),
                  pl.BlockSpec((k, TN_FF), lambda i, j: (0, j)),
                  pl.BlockSpec((k, TN_FF), lambda i, j: (0, j))],
        out_specs=pl.BlockSpec((TM, TN_FF), lambda i, j: (i, j)),
        out_shape=jax.ShapeDtypeStruct((rows, D_FF), BF16),
        compiler_params=_params("parallel", "arbitrary"),
        name="ffn_gate_up",
    )(h, wg, wu)


def _ffn_down_kernel(a_ref, w_ref, x_ref, o_ref):
    part = _dot(a_ref[...], w_ref[...])

    @pl.when(pl.program_id(2) == 0)
    def _():
        o_ref[...] = x_ref[...] + part

    @pl.when(pl.program_id(2) != 0)
    def _():
        o_ref[...] += part


def ffn_down(a, w, x):
    rows = x.shape[0]
    return pl.pallas_call(
        _ffn_down_kernel,
        grid=(rows // TM, D_MODEL // TN_DOWN, D_FF // TK_DOWN),
        in_specs=[pl.BlockSpec((TM, TK_DOWN), lambda i, j, k: (i, k)),
                  pl.BlockSpec((TK_DOWN, TN_DOWN), lambda i, j, k: (k, j)),
                  pl.BlockSpec((TM, TN_DOWN), lambda i, j, k: (i, j))],
        out_specs=pl.BlockSpec((TM, TN_DOWN), lambda i, j, k: (i, j)),
        out_shape=jax.ShapeDtypeStruct((rows, D_MODEL), F32),
        compiler_params=_params("parallel", "arbitrary", "arbitrary"),
        name="ffn_down",
    )(a, w, x)


def _ple_kernel(h_ref, p_ref, wg_ref, wp_ref, x_ref, o_ref):
    gate = _sigmoid(_dot(h_ref[...], wg_ref[...]))
    proj = _dot(p_ref[...].astype(BF16), wp_ref[...])
    o_ref[...] = x_ref[...] + gate * proj


def ple_update(h, p, wg, wp, x):
    rows, k = h.shape
    return pl.pallas_call(
        _ple_kernel,
        grid=(rows // TM, D_MODEL // TN_PLE),
        in_specs=[pl.BlockSpec((TM, k), lambda i, j: (i, 0)),
                  pl.BlockSpec((TM, D_PLE), lambda i, j: (i, 0)),
                  pl.BlockSpec((k, TN_PLE), lambda i, j: (0, j)),
                  pl.BlockSpec((D_PLE, TN_PLE), lambda i, j: (0, j)),
                  pl.BlockSpec((TM, TN_PLE), lambda i, j: (i, j))],
        out_specs=pl.BlockSpec((TM, TN_PLE), lambda i, j: (i, j)),
        out_shape=jax.ShapeDtypeStruct((rows, D_MODEL), F32),
        compiler_params=_params("parallel", "arbitrary"),
        name="ple_update",
    )(h, p, wg, wp, x)


def _swap_rope_halves(x):
    lane = lax.broadcasted_iota(jnp.int32, x.shape, 1)
    first = (lane % MLA_ROPE) < (MLA_ROPE // 2)
    return jnp.where(first, pltpu.roll(x, LANES - MLA_ROPE // 2, 1), pltpu.roll(x, MLA_ROPE // 2, 1))


def _mla_q_kernel(z_ref, g_ref, wt_ref, cos_ref, sin_ref, qt_ref):
    cq = _rms(z_ref[...], g_ref[...]).astype(BF16)
    qt = _dot_nt(wt_ref[...], cq)
    cos, sin = cos_ref[...], sin_ref[...]
    half = MLA_ROPE // 2
    for h in range(MLA_HEADS):
        base = h * MLA_QK
        x1 = qt[base + MLA_NOPE:base + MLA_NOPE + half]
        x2 = qt[base + MLA_NOPE + half:base + MLA_QK]
        qt_ref[h, :MLA_NOPE, :] = (qt[base:base + MLA_NOPE] * Q_SCALE).astype(BF16)
        qt_ref[h, MLA_NOPE:MLA_NOPE + half, :] = ((x1 * cos - x2 * sin) * Q_SCALE).astype(BF16)
        qt_ref[h, MLA_NOPE + half:, :] = ((x2 * cos + x1 * sin) * Q_SCALE).astype(BF16)


def mla_q(z, g, w_uq_t, cos_t, sin_t):
    rows = z.shape[0]
    tab_spec = pl.BlockSpec((MLA_ROPE // 2, TQ), lambda i: (0, i))
    return pl.pallas_call(
        _mla_q_kernel,
        grid=(rows // TQ,),
        in_specs=[pl.BlockSpec((TQ, MLA_Q_RANK), lambda i: (i, Z_CQ // MLA_Q_RANK)),
                  pl.BlockSpec((1, MLA_Q_RANK), lambda i: (0, 0)),
                  pl.BlockSpec(w_uq_t.shape, lambda i: (0, 0)),
                  tab_spec, tab_spec],
        out_specs=pl.BlockSpec((MLA_HEADS, MLA_QK, TQ), lambda i: (0, 0, i)),
        out_shape=jax.ShapeDtypeStruct((MLA_HEADS, MLA_QK, rows), BF16),
        compiler_params=_params("parallel"),
        name="mla_q",
    )(z, g.reshape(1, -1), w_uq_t, cos_t, sin_t)


def _mla_latent_kernel(zc_ref, zk_ref, g_ref, ck_ref, sk_ref, ckv_ref, kr_ref):
    ckv_ref[...] = _rms(zc_ref[...], g_ref[...])
    kr = zk_ref[...]
    rot = kr * ck_ref[...] + _swap_rope_halves(kr) * sk_ref[...]
    kr_ref[...] = rot[:, :MLA_ROPE]


def mla_latent(z, g, ck, sk):
    rows = z.shape[0]
    return pl.pallas_call(
        _mla_latent_kernel,
        grid=(rows // TR,),
        in_specs=[pl.BlockSpec((TR, MLA_KV_RANK), lambda i: (i, Z_CKV // MLA_KV_RANK)),
                  pl.BlockSpec((TR, LANES), lambda i: (i, Z_KR // LANES)),
                  pl.BlockSpec((1, MLA_KV_RANK), lambda i: (0, 0)),
                  pl.BlockSpec((TR, LANES), lambda i: (i, 0)),
                  pl.BlockSpec((TR, LANES), lambda i: (i, 0))],
        out_specs=[pl.BlockSpec((TR, MLA_KV_RANK), lambda i: (i, 0)),
                   pl.BlockSpec((TR, MLA_ROPE), lambda i: (i, 0))],
        out_shape=[jax.ShapeDtypeStruct((rows, MLA_KV_RANK), F32),
                   jax.ShapeDtypeStruct((rows, MLA_ROPE), F32)],
        compiler_params=_params("parallel"),
        name="mla_latent",
    )(z, z, g.reshape(1, -1), ck, sk)


def _kv_proj_kernel(ckv_ref, kr_ref, wk_ref, wvt_ref, k_ref, vt_ref):
    c = ckv_ref[...].astype(BF16)
    kn = _dot(c, wk_ref[...])
    vt = _dot_nt(wvt_ref[...], c)
    kr = kr_ref[...].astype(BF16)
    for h in range(MLA_HEADS):
        k_ref[h, :, :MLA_NOPE] = kn[:, h * MLA_NOPE:(h + 1) * MLA_NOPE].astype(BF16)
        k_ref[h, :, MLA_NOPE:] = kr
        vt_ref[h, 0] = vt[h * MLA_V:(h + 1) * MLA_V].astype(BF16)


def kv_proj(ckv, kr, w_uk, w_uv_t, tile):
    rows = ckv.shape[0]
    return pl.pallas_call(
        _kv_proj_kernel,
        grid=(rows // tile,),
        in_specs=[pl.BlockSpec((tile, MLA_KV_RANK), lambda i: (i, 0)),
                  pl.BlockSpec((tile, MLA_ROPE), lambda i: (i, 0)),
                  pl.BlockSpec(w_uk.shape, lambda i: (0, 0)),
                  pl.BlockSpec(w_uv_t.shape, lambda i: (0, 0))],
        out_specs=[pl.BlockSpec((MLA_HEADS, tile, MLA_QK), lambda i: (0, i, 0)),
                   pl.BlockSpec((MLA_HEADS, 1, MLA_V, tile), lambda i: (0, i, 0, 0))],
        out_shape=[jax.ShapeDtypeStruct((MLA_HEADS, rows, MLA_QK), BF16),
                   jax.ShapeDtypeStruct((MLA_HEADS, rows // tile, MLA_V, tile), BF16)],
        compiler_params=_params("parallel"),
        name="kv_proj",
    )(ckv, kr, w_uk, w_uv_t)


def _attn_prompt_kernel(qt_ref, k_ref, vt_ref, o_ref, m_sc, l_sc, acc_sc):
    i = pl.program_id(1)
    m_sc[...] = jnp.full(m_sc.shape, -jnp.inf, F32)
    l_sc[...] = jnp.zeros(l_sc.shape, F32)
    acc_sc[...] = jnp.zeros(acc_sc.shape, F32)

    def step(j, masked):
        start = pl.multiple_of(j * ATT_T, ATT_T)
        for hb in range(ATT_HB):
            s = _dot(k_ref[hb, pl.ds(start, ATT_T), :], qt_ref[hb])
            if masked:
                kc = lax.broadcasted_iota(jnp.int32, s.shape, 0) // CHUNK
                qc = lax.broadcasted_iota(jnp.int32, s.shape, 1) // CHUNK
                s = jnp.where(kc <= qc, s, -jnp.inf)
            m_prev = m_sc[hb]
            m_new = jnp.maximum(m_prev, jnp.max(s, axis=0, keepdims=True))
            alpha = jnp.exp2(m_prev - m_new)
            p = jnp.exp2(s - m_new)
            l_sc[hb] = alpha * l_sc[hb] + jnp.sum(p, axis=0, keepdims=True)
            acc_sc[hb] = alpha * acc_sc[hb] + _dot(vt_ref[hb, j], p.astype(BF16))
            m_sc[hb] = m_new

    def body(j, carry):
        step(j, False)
        return carry

    lax.fori_loop(0, i, body, 0)
    step(i, True)
    for hb in range(ATT_HB):
        out_t = acc_sc[hb] / l_sc[hb]
        o_ref[:, hb * MLA_V:(hb + 1) * MLA_V] = out_t.T.astype(o_ref.dtype)


def attn_prompt(qt, k, vt):
    nblk = SEQ // ATT_T
    return pl.pallas_call(
        _attn_prompt_kernel,
        grid=(MLA_HEADS // ATT_HB, nblk),
        in_specs=[pl.BlockSpec((ATT_HB, MLA_QK, ATT_T), lambda g, i: (g, 0, i)),
                  pl.BlockSpec((ATT_HB, SEQ, MLA_QK), lambda g, i: (g, 0, 0)),
                  pl.BlockSpec((ATT_HB, nblk, MLA_V, ATT_T), lambda g, i: (g, 0, 0, 0))],
        out_specs=pl.BlockSpec((ATT_T, ATT_HB * MLA_V), lambda g, i: (i, g)),
        out_shape=jax.ShapeDtypeStruct((SEQ, MLA_HEADS * MLA_V), BF16),
        scratch_shapes=[pltpu.VMEM((ATT_HB, 1, ATT_T), F32), pltpu.VMEM((ATT_HB, 1, ATT_T), F32),
                        pltpu.VMEM((ATT_HB, MLA_V, ATT_T), F32)],
        compiler_params=_params("parallel", "arbitrary"),
        name="attn_prompt",
    )(qt, k, vt)


def _attn_sample_kernel(qt_ref, k_ref, vt_ref, o_ref):
    qt = qt_ref[0]
    shape = (SAMPLE_KEYS_PAD, N_SAMPLE)
    kpos = lax.broadcasted_iota(jnp.int32, shape, 0)
    qpos = PAST_LEN + lax.broadcasted_iota(jnp.int32, shape, 1) % DEC_SEQ
    kchunk = jnp.where(kpos < SAMPLE_KEYS, kpos // CHUNK, SAMPLE_KEYS_PAD)
    visible = kchunk <= qpos // CHUNK
    stream = lax.broadcasted_iota(jnp.int32, (MLA_V, N_SAMPLE), 1) // DEC_SEQ
    out_t = jnp.zeros((MLA_V, N_SAMPLE), F32)
    for b in range(DEC_BATCH):
        s = jnp.where(visible, _dot(k_ref[0, b], qt), -jnp.inf)
        p = jnp.exp2(s - jnp.max(s, axis=0, keepdims=True))
        o_b = _dot(vt_ref[0, b], p.astype(BF16)) / jnp.sum(p, axis=0, keepdims=True)
        out_t = jnp.where(stream == b, o_b, out_t)
    o_ref[...] = out_t.T.astype(o_ref.dtype)


def attn_sample(qt, k, vt):
    return pl.pallas_call(
        _attn_sample_kernel,
        grid=(MLA_HEADS,),
        in_specs=[pl.BlockSpec((1, MLA_QK, N_SAMPLE), lambda h: (h, 0, SEQ // N_SAMPLE)),
                  pl.BlockSpec((1, DEC_BATCH, SAMPLE_KEYS_PAD, MLA_QK), lambda h: (h, 0, 0, 0)),
                  pl.BlockSpec((1, DEC_BATCH, MLA_V, SAMPLE_KEYS_PAD), lambda h: (h, 0, 0, 0))],
        out_specs=pl.BlockSpec((N_SAMPLE, MLA_V), lambda h: (0, h)),
        out_shape=jax.ShapeDtypeStruct((N_SAMPLE, MLA_HEADS * MLA_V), BF16),
        compiler_params=_params("parallel"),
        name="attn_sample",
    )(qt, k, vt)


CONV_PAD = 32
CONV_OFF = CONV_PAD - CONV_HIST


def _conv_kernel(za_ref, zb_ref, past_ref, w_ref, b_ref, g_ref, beta_ref, o_ref, hist_ref, ext_sc, *, t_rows):
    @pl.when(pl.program_id(1) == 0)
    def _():
        ext_sc[CONV_OFF:CONV_PAD, :] = past_ref[0]

    u = za_ref[...] * _sigmoid(zb_ref[...])
    ext_sc[CONV_PAD:CONV_PAD + t_rows, :] = u
    cols = []
    for c0 in range(0, CONV_CH, LANES):
        acc = jnp.zeros((t_rows, LANES), F32)
        for tap in range(CONV_WIDTH):
            acc += w_ref[tap:tap + 1, c0:c0 + LANES] * ext_sc[CONV_OFF + tap:CONV_OFF + tap + t_rows, c0:c0 + LANES]
        cols.append(acc)
    y = jnp.concatenate(cols, axis=1) + b_ref[...]
    yc = y - jnp.mean(y, axis=-1, keepdims=True)
    y = yc * lax.rsqrt(jnp.mean(yc * yc, axis=-1, keepdims=True) + EPS) * g_ref[...] + beta_ref[...]
    o_ref[...] = (y * _sigmoid(y)).astype(o_ref.dtype)
    tail = ext_sc[t_rows + CONV_OFF:t_rows + CONV_PAD, :]
    hist_ref[0] = tail
    ext_sc[CONV_OFF:CONV_PAD, :] = tail


def conv_mixer(z, past, w, b, g, beta, *, batch, seq, row_off, t_rows):
    steps = seq // t_rows
    first = row_off // t_rows

    def rows(bi, t):
        return first + bi * steps + t

    vec = pl.BlockSpec((1, CONV_CH), lambda bi, t: (0, 0))
    return pl.pallas_call(
        functools.partial(_conv_kernel, t_rows=t_rows),
        grid=(batch, steps),
        in_specs=[pl.BlockSpec((t_rows, CONV_CH), lambda bi, t: (rows(bi, t), Z_CA // CONV_CH)),
                  pl.BlockSpec((t_rows, CONV_CH), lambda bi, t: (rows(bi, t), Z_CB // CONV_CH)),
                  pl.BlockSpec((1, CONV_HIST, CONV_CH), lambda bi, t: (bi, 0, 0)),
                  pl.BlockSpec((CONV_WIDTH, CONV_CH), lambda bi, t: (0, 0)),
                  vec, vec, vec],
        out_specs=[pl.BlockSpec((t_rows, CONV_CH), lambda bi, t: (bi * steps + t, 0)),
                   pl.BlockSpec((1, CONV_HIST, CONV_CH), lambda bi, t: (bi, 0, 0))],
        out_shape=[jax.ShapeDtypeStruct((batch * seq, CONV_CH), BF16),
                   jax.ShapeDtypeStruct((batch, CONV_HIST, CONV_CH), F32)],
        scratch_shapes=[pltpu.VMEM((CONV_PAD + t_rows, CONV_CH), F32)],
        compiler_params=_params("parallel", "arbitrary"),
        name="conv_mixer",
    )(z, z, past, w, b.reshape(1, -1), g.reshape(1, -1), beta.reshape(1, -1))


def _pad_rows(x, rows):
    if x.shape[0] == rows:
        return x
    return jnp.concatenate([x, jnp.zeros((rows - x.shape[0], x.shape[1]), x.dtype)], axis=0)


def _log_sigmoid(x):
    return jnp.minimum(x, 0.0) - jnp.log1p(jnp.exp(-jnp.abs(x)))


def _gla_kernel(q_ref, k_ref, v_ref, r_ref, al_ref, wa_ref, ba_ref, gn_ref, s0_ref, o_ref, s_out_ref, s_sc,
                *, c, cps):
    step = pl.program_id(1)

    @pl.when(step == 0)
    def _():
        s_sc[...] = s0_ref[0]

    row = lax.broadcasted_iota(jnp.int32, (GLA_CP, GLA_CP), 0)
    col = lax.broadcasted_iota(jnp.int32, (GLA_CP, GLA_CP), 1)
    causal = col <= row
    tri = causal.astype(F32)
    for ci in range(cps):
        r0 = ci * c
        gate = _dot(al_ref[r0:r0 + c, :].astype(BF16), wa_ref[...]) + ba_ref[...]
        log_a = _pad_rows(_log_sigmoid(gate) / GLA_TAU, GLA_CP)
        bcum = jnp.dot(tri, log_a, precision=lax.Precision.HIGHEST, preferred_element_type=F32)
        for h in range(GLA_HEADS):
            ks = slice(h * GLA_DK, (h + 1) * GLA_DK)
            vs = slice(h * GLA_DV, (h + 1) * GLA_DV)
            b = bcum[:, ks]
            q = _pad_rows(q_ref[r0:r0 + c, ks], GLA_CP) * GLA_DK ** -0.5
            k = _pad_rows(k_ref[r0:r0 + c, ks], GLA_CP)
            v = _pad_rows(v_ref[r0:r0 + c, vs], GLA_CP).astype(BF16)
            q_t = (q * jnp.exp(b)).astype(BF16)
            k_t = (k * jnp.exp(-b)).astype(BF16)
            a = jnp.where(causal, _dot_nt(q_t, k_t), 0.0)
            s_prev = s_sc[h]
            o = _dot(q_t, s_prev.astype(BF16)) + _dot(a.astype(BF16), v)
            b_last = b[c - 1:c, :]
            k_dec = k * jnp.exp(b_last - b)
            k_dec_t = _pad_rows(k_dec, XPOSE).T
            decay = jnp.exp(_pad_rows(b, XPOSE).T[:, c - 1:c])
            s_sc[h] = decay * s_prev + _dot(k_dec_t.astype(BF16), _pad_rows(v, XPOSE))
            on = _rms(o[:c], gn_ref[...])
            r = r_ref[r0:r0 + c, vs]
            o_ref[r0:r0 + c, vs] = (on * (r * _sigmoid(r))).astype(o_ref.dtype)

    @pl.when(step == pl.num_programs(1) - 1)
    def _():
        s_out_ref[0] = s_sc[...]


def gla_mixer(z, s0, wa2, ba, gn, *, batch, seq, row_off, c, cps):
    t_rows = c * cps
    steps = seq // t_rows
    first = row_off // t_rows

    def rows(bi, t):
        return first + bi * steps + t

    state = pl.BlockSpec((1, GLA_HEADS, GLA_DK, GLA_DV), lambda bi, t: (bi, 0, 0, 0))
    return pl.pallas_call(
        functools.partial(_gla_kernel, c=c, cps=cps),
        grid=(batch, steps),
        in_specs=[pl.BlockSpec((t_rows, HK), lambda bi, t: (rows(bi, t), Z_GQ // HK)),
                  pl.BlockSpec((t_rows, HK), lambda bi, t: (rows(bi, t), Z_GK // HK)),
                  pl.BlockSpec((t_rows, HV), lambda bi, t: (rows(bi, t), Z_GV // HV)),
                  pl.BlockSpec((t_rows, HV), lambda bi, t: (rows(bi, t), Z_GR // HV)),
                  pl.BlockSpec((t_rows, LANES), lambda bi, t: (rows(bi, t), Z_AL // LANES)),
                  pl.BlockSpec((LANES, HK), lambda bi, t: (0, 0)),
                  pl.BlockSpec((1, HK), lambda bi, t: (0, 0)),
                  pl.BlockSpec((1, GLA_DV), lambda bi, t: (0, 0)),
                  state],
        out_specs=[pl.BlockSpec((t_rows, HV), lambda bi, t: (bi * steps + t, 0)), state],
        out_shape=[jax.ShapeDtypeStruct((batch * seq, HV), BF16),
                   jax.ShapeDtypeStruct((batch, GLA_HEADS, GLA_DK, GLA_DV), F32)],
        scratch_shapes=[pltpu.VMEM((GLA_HEADS, GLA_DK, GLA_DV), F32)],
        compiler_params=_params("parallel", "arbitrary"),
        name="gla_mixer",
    )(z, z, z, z, z, wa2, ba.reshape(1, -1), gn.reshape(1, -1), s0)


def _pack_w_in(w_in):
    w = w_in.astype(BF16)
    mla, conv = 0, MLA_Q_RANK + MLA_KV_RANK + MLA_ROPE
    gla = conv + 2 * CONV_CH

    def cols(a, n):
        return w[..., a:a + n]

    def zeros(n):
        return jnp.zeros(w.shape[:-1] + (n,), BF16)

    return jnp.concatenate([
        cols(conv, CONV_CH), cols(conv + CONV_CH, CONV_CH),
        cols(gla + 2 * HK, HV), cols(gla + 2 * HK + HV, HV),
        cols(mla, MLA_Q_RANK),
        cols(gla, HK), cols(gla + HK, HK),
        cols(mla + MLA_Q_RANK, MLA_KV_RANK),
        cols(mla + MLA_Q_RANK + MLA_KV_RANK, MLA_ROPE), zeros(LANES - MLA_ROPE),
        cols(gla + 2 * HK + 2 * HV, GLA_GATE_RANK), zeros(LANES - GLA_GATE_RANK),
    ], axis=-1)


def _rope_tables():
    half = MLA_ROPE // 2
    pos = jnp.concatenate([jnp.arange(SEQ, dtype=jnp.int32),
                           jnp.tile(PAST_LEN + jnp.arange(DEC_SEQ, dtype=jnp.int32), DEC_BATCH)])
    inv_freq = ROPE_THETA ** (-jnp.arange(half, dtype=F32) / half)
    ang = pos.astype(F32)[:, None] * inv_freq[None, :]
    cos, sin = jnp.cos(ang), jnp.sin(ang)
    c64 = jnp.concatenate([cos, cos], axis=1)
    s64 = jnp.concatenate([-sin, sin], axis=1)
    zero = jnp.zeros_like(s64)
    key = (jnp.concatenate([c64, zero], axis=1), jnp.concatenate([s64, zero], axis=1))
    return (cos.T, sin.T), key


@jax.jit
def _forward(x_prompt, x_sample, cache_ckv, cache_krope, cache_conv, state_gla, p_prompt, p_sample, norm_mix,
             w_in, mla_q_norm, mla_kv_norm, mla_w_uq, mla_w_uk, mla_w_uv, conv_dw_w, conv_dw_b, conv_ln_g,
             conv_ln_b, gla_w_a2, gla_b_a, gla_norm, w_out, norm_ffn, ffn_w_gate, ffn_w_up, ffn_w_down,
             norm_ple, ple_w_gate, ple_w_proj, norm_final):
    w_in_p = _pack_w_in(w_in)
    w_uq_t = mla_w_uq.astype(BF16).transpose(0, 2, 1)
    w_uk = mla_w_uk.reshape(DEPTH, MLA_KV_RANK, MLA_HEADS * MLA_NOPE).astype(BF16)
    w_uv_t = mla_w_uv.reshape(DEPTH, MLA_KV_RANK, MLA_HEADS * MLA_V).astype(BF16).transpose(0, 2, 1)
    w_a2 = jnp.concatenate([gla_w_a2.astype(BF16),
                            jnp.zeros((DEPTH, LANES - GLA_GATE_RANK, HK), BF16)], axis=1)
    w_out_b = w_out.astype(BF16)
    w_gate, w_up, w_down = ffn_w_gate.astype(BF16), ffn_w_up.astype(BF16), ffn_w_down.astype(BF16)
    w_pg, w_pp = ple_w_gate.astype(BF16), ple_w_proj.astype(BF16)
    (cos_t, sin_t), (ck, sk) = _rope_tables()

    x = jnp.concatenate([x_prompt.reshape(SEQ, D_MODEL), x_sample.reshape(N_SAMPLE, D_MODEL)], axis=0)
    p_all = jnp.concatenate([p_prompt.reshape(DEPTH, SEQ, D_PLE), p_sample.reshape(DEPTH, N_SAMPLE, D_PLE)], axis=1)
    zero_conv = jnp.zeros((1, CONV_HIST, CONV_CH), F32)
    zero_gla = jnp.zeros((1, GLA_HEADS, GLA_DK, GLA_DV), F32)

    ckvs, krs, convs_p, convs_s, glas_p, glas_s = [], [], [], [], [], []
    for i in range(DEPTH):
        h = rmsnorm_rows(x, norm_mix[i], BF16)
        z = in_proj(h, w_in_p[i])

        qt = mla_q(z, mla_q_norm[i], w_uq_t[i], cos_t, sin_t)
        ckv, kr = mla_latent(z, mla_kv_norm[i], ck, sk)
        k_p, vt_p = kv_proj(ckv[:SEQ], kr[:SEQ], w_uk[i], w_uv_t[i], ATT_T)
        pad = SAMPLE_KEYS_PAD - SAMPLE_KEYS
        ckv_s = jnp.concatenate([cache_ckv[i], ckv[SEQ:].reshape(DEC_BATCH, DEC_SEQ, MLA_KV_RANK),
                                 jnp.zeros((DEC_BATCH, pad, MLA_KV_RANK), F32)], axis=1)
        kr_s = jnp.concatenate([cache_krope[i], kr[SEQ:].reshape(DEC_BATCH, DEC_SEQ, MLA_ROPE),
                                jnp.zeros((DEC_BATCH, pad, MLA_ROPE), F32)], axis=1)
        k_s, vt_s = kv_proj(ckv_s.reshape(-1, MLA_KV_RANK), kr_s.reshape(-1, MLA_ROPE), w_uk[i], w_uv_t[i],
                            SAMPLE_KEYS_PAD)
        k_s = k_s.reshape(MLA_HEADS, DEC_BATCH, SAMPLE_KEYS_PAD, MLA_QK)
        o_a = jnp.concatenate([attn_prompt(qt, k_p, vt_p), attn_sample(qt, k_s, vt_s)], axis=0)

        conv_args = (conv_dw_w[i], conv_dw_b[i], conv_ln_g[i], conv_ln_b[i])
        ob_p, conv_p = conv_mixer(z, zero_conv, *conv_args, batch=1, seq=SEQ, row_off=0, t_rows=CONV_T)
        ob_s, conv_s = conv_mixer(z, cache_conv[i], *conv_args, batch=DEC_BATCH, seq=DEC_SEQ, row_off=SEQ,
                                  t_rows=DEC_SEQ)
        o_b = jnp.concatenate([ob_p, ob_s], axis=0)

        gla_args = (w_a2[i], gla_b_a[i], gla_norm[i])
        oc_p, gla_p = gla_mixer(z, zero_gla, *gla_args, batch=1, seq=SEQ, row_off=0, c=CHUNK, cps=GLA_CPS)
        oc_s, gla_s = gla_mixer(z, state_gla[i], *gla_args, batch=DEC_BATCH, seq=DEC_SEQ, row_off=SEQ,
                                c=DEC_SEQ, cps=1)
        o_c = jnp.concatenate([oc_p, oc_s], axis=0)

        x = out_proj(o_a, o_b, o_c, w_out_b[i], x)
        h = rmsnorm_rows(x, norm_ffn[i], BF16)
        x = ffn_down(ffn_gate_up(h, w_gate[i], w_up[i]), w_down[i], x)
        h = rmsnorm_rows(x, norm_ple[i], BF16)
        x = ple_update(h, p_all[i], w_pg[i], w_pp[i], x)

        ckvs.append(ckv)
        krs.append(kr)
        convs_p.append(conv_p)
        convs_s.append(conv_s)
        glas_p.append(gla_p)
        glas_s.append(gla_s)

    y = rmsnorm_rows(x, norm_final, F32)
    ckv_all, kr_all = jnp.stack(ckvs), jnp.stack(krs)
    return (y[:SEQ].reshape(1, SEQ, D_MODEL),
            y[SEQ:].reshape(DEC_BATCH, DEC_SEQ, D_MODEL),
            ckv_all[:, :SEQ].reshape(DEPTH, 1, SEQ, MLA_KV_RANK),
            kr_all[:, :SEQ].reshape(DEPTH, 1, SEQ, MLA_ROPE),
            jnp.stack(convs_p),
            jnp.stack(glas_p),
            ckv_all[:, SEQ:].reshape(DEPTH, DEC_BATCH, DEC_SEQ, MLA_KV_RANK),
            kr_all[:, SEQ:].reshape(DEPTH, DEC_BATCH, DEC_SEQ, MLA_ROPE),
            jnp.stack(convs_s),
            jnp.stack(glas_s))


def kernel(x_prompt, x_sample, cache_ckv, cache_krope, cache_conv, state_gla, p_prompt, p_sample, norm_mix, w_in, mla_q_norm, mla_kv_norm, mla_w_uq, mla_w_uk, mla_w_uv, conv_dw_w, conv_dw_b, conv_ln_g, conv_ln_b, gla_w_a2, gla_b_a, gla_norm, w_out, norm_ffn, ffn_w_gate, ffn_w_up, ffn_w_down, norm_ple, ple_w_gate, ple_w_proj, norm_final):
    return _forward(x_prompt, x_sample, cache_ckv, cache_krope, cache_conv, state_gla, p_prompt, p_sample,
                    norm_mix, w_in, mla_q_norm, mla_kv_norm, mla_w_uq, mla_w_uk, mla_w_uv, conv_dw_w, conv_dw_b,
                    conv_ln_g, conv_ln_b, gla_w_a2, gla_b_a, gla_norm, w_out, norm_ffn, ffn_w_gate, ffn_w_up,
                    ffn_w_down, norm_ple, ple_w_gate, ple_w_proj, norm_final)
```

```python
import functools

import jax
import jax.numpy as jnp
from jax import lax
from jax.experimental import pallas as pl
from jax.experimental.pallas import tpu as pltpu

F32 = jnp.float32
BF16 = jnp.bfloat16

D_MODEL = 4096
SEQ = 8192
DEPTH = 4
DEC_BATCH = 8
DEC_SEQ = 16
PAST_LEN = 1024
CHUNK = 64
EPS = 1e-6
D_PLE = 256
D_FF = 11008
MLA_HEADS = 16
MLA_Q_RANK = 1024
MLA_KV_RANK = 512
MLA_NOPE = 128
MLA_ROPE = 64
MLA_V = 128
MLA_QK = MLA_NOPE + MLA_ROPE
ROPE_THETA = 10000.0
CONV_CH = 1024
CONV_WIDTH = 31
CONV_HIST = CONV_WIDTH - 1
GLA_HEADS = 4
GLA_DK = 128
GLA_DV = 256
GLA_GATE_RANK = 16
GLA_TAU = 16.0
HK = GLA_HEADS * GLA_DK
HV = GLA_HEADS * GLA_DV

N_SAMPLE = DEC_BATCH * DEC_SEQ
ROWS = SEQ + N_SAMPLE
SAMPLE_KEYS = PAST_LEN + DEC_SEQ
SAMPLE_KEYS_PAD = 1152

LANES = 128
SUBLANES = 8
VMEM_LIMIT = 52 * 1024 * 1024

Z_CA = 0
Z_CB = 1024
Z_GV = 2048
Z_GR = 3072
Z_CQ = 4096
Z_GQ = 5120
Z_GK = 5632
Z_CKV = 6144
Z_KR = 6656
Z_AL = 6784
Z_W = 6912

TM = 832
TR = 416
TN_IN = 768
TN_OUT = 512
TN_FF = 256
TK_DOWN = D_FF // 2
TN_DOWN = 512
TN_PLE = 512
TQ = 640
ATT_T = 512
ATT_HB = 2
Q_SCALE = MLA_QK ** -0.5 * 1.4426950408889634
CONV_T = 128
GLA_CPS = 4
GLA_CP = 64
XPOSE = 128


def _params(*sem):
    return pltpu.CompilerParams(dimension_semantics=sem, vmem_limit_bytes=VMEM_LIMIT)


def _rms(x, g):
    return x * lax.rsqrt(jnp.mean(x * x, axis=-1, keepdims=True) + EPS) * g


def _sigmoid(x):
    return 1.0 / (1.0 + jnp.exp(-x))


def _dot(a, b):
    return jnp.dot(a, b, preferred_element_type=F32)


def _dot_nt(a, b):
    return lax.dot_general(a, b, (((1,), (1,)), ((), ())), preferred_element_type=F32)


def _rmsnorm_kernel(x_ref, g_ref, o_ref):
    o_ref[...] = _rms(x_ref[...], g_ref[...]).astype(o_ref.dtype)


def rmsnorm_rows(x, g, out_dtype):
    rows, d = x.shape
    return pl.pallas_call(
        _rmsnorm_kernel,
        grid=(rows // TR,),
        in_specs=[pl.BlockSpec((TR, d), lambda i: (i, 0)),
                  pl.BlockSpec((1, d), lambda i: (0, 0))],
        out_specs=pl.BlockSpec((TR, d), lambda i: (i, 0)),
        out_shape=jax.ShapeDtypeStruct((rows, d), out_dtype),
        compiler_params=_params("parallel"),
        name="rmsnorm_rows",
    )(x, g.reshape(1, d))


def _matmul_kernel(a_ref, b_ref, o_ref):
    o_ref[...] = _dot(a_ref[...], b_ref[...]).astype(o_ref.dtype)


def in_proj(h, w, layer):
    rows, k = h.shape
    n = w.shape[2]
    return pl.pallas_call(
        _matmul_kernel,
        grid=(rows // TM, n // TN_IN),
        in_specs=[pl.BlockSpec((TM, k), lambda i, j: (i, 0)),
                  pl.BlockSpec((None, k, TN_IN), lambda i, j: (layer, 0, j))],
        out_specs=pl.BlockSpec((TM, TN_IN), lambda i, j: (i, j)),
        out_shape=jax.ShapeDtypeStruct((rows, n), F32),
        compiler_params=_params("parallel", "arbitrary"),
        name="in_proj",
    )(h, w)


def _out_proj_kernel(oa_ref, ob_ref, oc_ref, wa_ref, wb_ref, wc_ref, x_ref, o_ref):
    acc = _dot(oa_ref[...], wa_ref[...])
    acc += _dot(ob_ref[...], wb_ref[...])
    acc += _dot(oc_ref[...], wc_ref[...])
    o_ref[...] = x_ref[...] + acc


def out_proj(o_a, o_b, o_c, w, x, layer):
    rows = x.shape[0]
    da, db, dc = o_a.shape[1], o_b.shape[1], o_c.shape[1]
    return pl.pallas_call(
        _out_proj_kernel,
        grid=(rows // TM, D_MODEL // TN_OUT),
        in_specs=[pl.BlockSpec((TM, da), lambda i, j: (i, 0)),
                  pl.BlockSpec((TM, db), lambda i, j: (i, 0)),
                  pl.BlockSpec((TM, dc), lambda i, j: (i, 0)),
                  pl.BlockSpec((None, da, TN_OUT), lambda i, j: (layer, 0, j)),
                  pl.BlockSpec((None, db, TN_OUT), lambda i, j: (layer, da // db, j)),
                  pl.BlockSpec((None, dc, TN_OUT), lambda i, j: (layer, (da + db) // dc, j)),
                  pl.BlockSpec((TM, TN_OUT), lambda i, j: (i, j))],
        out_specs=pl.BlockSpec((TM, TN_OUT), lambda i, j: (i, j)),
        out_shape=jax.ShapeDtypeStruct((rows, D_MODEL), F32),
        compiler_params=_params("parallel", "arbitrary"),
        name="out_proj",
    )(o_a, o_b, o_c, w, w, w, x)


def _ffn_gate_up_kernel(h_ref, wg_ref, wu_ref, o_ref):
    h = h_ref[...]
    g = _dot(h, wg_ref[...])
    u = _dot(h, wu_ref[...])
    o_ref[...] = (g * _sigmoid(g) * u).astype(o_ref.dtype)


def ffn_gate_up(h, wg, wu, layer):
    rows, k = h.shape
    return pl.pallas_call(
        _ffn_gate_up_kernel,
        grid=(rows // TM, D_FF // TN_FF),
        in_specs=[pl.BlockSpec((TM, k), lambda i, j: (i, 0)),
                  pl.BlockSpec((None, k, TN_FF), lambda i, j: (layer, 0, j)),
                  pl.BlockSpec((None, k, TN_FF), lambda i, j: (layer, 0, j))],
        out_specs=pl.BlockSpec((TM, TN_FF), lambda i, j: (i, j)),
        out_shape=jax.ShapeDtypeStruct((rows, D_FF), BF16),
        compiler_params=_params("parallel", "arbitrary"),
        name="ffn_gate_up",
    )(h, wg, wu)


def _ffn_down_kernel(a_ref, w_ref, x_ref, o_ref):
    part = _dot(a_ref[...], w_ref[...])

    @pl.when(pl.program_id(2) == 0)
    def _():
        o_ref[...] = x_ref[...] + part

    @pl.when(pl.program_id(2) != 0)
    def _():
        o_ref[...] += part


def ffn_down(a, w, x, layer):
    rows = x.shape[0]
    return pl.pallas_call(
        _ffn_down_kernel,
        grid=(rows // TM, D_MODEL // TN_DOWN, D_FF // TK_DOWN),
        in_specs=[pl.BlockSpec((TM, TK_DOWN), lambda i, j, k: (i, k)),
                  pl.BlockSpec((None, TK_DOWN, TN_DOWN), lambda i, j, k: (layer, k, j)),
                  pl.BlockSpec((TM, TN_DOWN), lambda i, j, k: (i, j))],
        out_specs=pl.BlockSpec((TM, TN_DOWN), lambda i, j, k: (i, j)),
        out_shape=jax.ShapeDtypeStruct((rows, D_MODEL), F32),
        compiler_params=_params("parallel", "arbitrary", "arbitrary"),
        name="ffn_down",
    )(a, w, x)


def _ple_kernel(h_ref, p_ref, wg_ref, wp_ref, x_ref, o_ref):
    gate = _sigmoid(_dot(h_ref[...], wg_ref[...]))
    proj = _dot(p_ref[...].astype(BF16), wp_ref[...])
    o_ref[...] = x_ref[...] + gate * proj


def ple_update(h, p, wg, wp, x, layer):
    rows, k = h.shape
    return pl.pallas_call(
        _ple_kernel,
        grid=(rows // TM, D_MODEL // TN_PLE),
        in_specs=[pl.BlockSpec((TM, k), lambda i, j: (i, 0)),
                  pl.BlockSpec((None, TM, D_PLE), lambda i, j: (layer, i, 0)),
                  pl.BlockSpec((None, k, TN_PLE), lambda i, j: (layer, 0, j)),
                  pl.BlockSpec((None, D_PLE, TN_PLE), lambda i, j: (layer, 0, j)),
                  pl.BlockSpec((TM, TN_PLE), lambda i, j: (i, j))],
        out_specs=pl.BlockSpec((TM, TN_PLE), lambda i, j: (i, j)),
        out_shape=jax.ShapeDtypeStruct((rows, D_MODEL), F32),
        compiler_params=_params("parallel", "arbitrary"),
        name="ple_update",
    )(h, p, wg, wp, x)


def _swap_rope_halves(x):
    lane = lax.broadcasted_iota(jnp.int32, x.shape, 1)
    first = (lane % MLA_ROPE) < (MLA_ROPE // 2)
    return jnp.where(first, pltpu.roll(x, LANES - MLA_ROPE // 2, 1), pltpu.roll(x, MLA_ROPE // 2, 1))


def _mla_q_kernel(z_ref, g_ref, wt_ref, cos_ref, sin_ref, qt_ref):
    cq = _rms(z_ref[...], g_ref[...]).astype(BF16)
    qt = _dot_nt(wt_ref[...], cq)
    cos, sin = cos_ref[...], sin_ref[...]
    half = MLA_ROPE // 2
    for h in range(MLA_HEADS):
        base = h * MLA_QK
        x1 = qt[base + MLA_NOPE:base + MLA_NOPE + half]
        x2 = qt[base + MLA_NOPE + half:base + MLA_QK]
        qt_ref[h, :MLA_NOPE, :] = (qt[base:base + MLA_NOPE] * Q_SCALE).astype(BF16)
        qt_ref[h, MLA_NOPE:MLA_NOPE + half, :] = ((x1 * cos - x2 * sin) * Q_SCALE).astype(BF16)
        qt_ref[h, MLA_NOPE + half:, :] = ((x2 * cos + x1 * sin) * Q_SCALE).astype(BF16)


def mla_q(z, g, w_uq_t, cos_t, sin_t):
    rows = z.shape[0]
    tab_spec = pl.BlockSpec((MLA_ROPE // 2, TQ), lambda i: (0, i))
    return pl.pallas_call(
        _mla_q_kernel,
        grid=(rows // TQ,),
        in_specs=[pl.BlockSpec((TQ, MLA_Q_RANK), lambda i: (i, Z_CQ // MLA_Q_RANK)),
                  pl.BlockSpec((1, MLA_Q_RANK), lambda i: (0, 0)),
                  pl.BlockSpec(w_uq_t.shape, lambda i: (0, 0)),
                  tab_spec, tab_spec],
        out_specs=pl.BlockSpec((MLA_HEADS, MLA_QK, TQ), lambda i: (0, 0, i)),
        out_shape=jax.ShapeDtypeStruct((MLA_HEADS, MLA_QK, rows), BF16),
        compiler_params=_params("parallel"),
        name="mla_q",
    )(z, g.reshape(1, -1), w_uq_t, cos_t, sin_t)


def _mla_latent_kernel(zc_ref, zk_ref, g_ref, ck_ref, sk_ref, ckv_ref, kr_ref):
    ckv_ref[...] = _rms(zc_ref[...], g_ref[...])
    kr = zk_ref[...]
    rot = kr * ck_ref[...] + _swap_rope_halves(kr) * sk_ref[...]
    kr_ref[...] = rot[:, :MLA_ROPE]


def mla_latent(z, g, ck, sk):
    rows = z.shape[0]
    return pl.pallas_call(
        _mla_latent_kernel,
        grid=(rows // TR,),
        in_specs=[pl.BlockSpec((TR, MLA_KV_RANK), lambda i: (i, Z_CKV // MLA_KV_RANK)),
                  pl.BlockSpec((TR, LANES), lambda i: (i, Z_KR // LANES)),
                  pl.BlockSpec((1, MLA_KV_RANK), lambda i: (0, 0)),
                  pl.BlockSpec((TR, LANES), lambda i: (i, 0)),
                  pl.BlockSpec((TR, LANES), lambda i: (i, 0))],
        out_specs=[pl.BlockSpec((TR, MLA_KV_RANK), lambda i: (i, 0)),
                   pl.BlockSpec((TR, MLA_ROPE), lambda i: (i, 0))],
        out_shape=[jax.ShapeDtypeStruct((rows, MLA_KV_RANK), F32),
                   jax.ShapeDtypeStruct((rows, MLA_ROPE), F32)],
        compiler_params=_params("parallel"),
        name="mla_latent",
    )(z, z, g.reshape(1, -1), ck, sk)


def _kv_proj_kernel(ckv_ref, kr_ref, wk_ref, wvt_ref, k_ref, vt_ref):
    c = ckv_ref[...].astype(BF16)
    kn = _dot(c, wk_ref[...])
    vt = _dot_nt(wvt_ref[...], c)
    kr = kr_ref[...].astype(BF16)
    for h in range(MLA_HEADS):
        k_ref[h, :, :MLA_NOPE] = kn[:, h * MLA_NOPE:(h + 1) * MLA_NOPE].astype(BF16)
        k_ref[h, :, MLA_NOPE:] = kr
        vt_ref[h, 0] = vt[h * MLA_V:(h + 1) * MLA_V].astype(BF16)


def kv_proj(ckv, kr, w_uk, w_uv_t, tile):
    rows = ckv.shape[0]
    return pl.pallas_call(
        _kv_proj_kernel,
        grid=(rows // tile,),
        in_specs=[pl.BlockSpec((tile, MLA_KV_RANK), lambda i: (i, 0)),
                  pl.BlockSpec((tile, MLA_ROPE), lambda i: (i, 0)),
                  pl.BlockSpec(w_uk.shape, lambda i: (0, 0)),
                  pl.BlockSpec(w_uv_t.shape, lambda i: (0, 0))],
        out_specs=[pl.BlockSpec((MLA_HEADS, tile, MLA_QK), lambda i: (0, i, 0)),
                   pl.BlockSpec((MLA_HEADS, 1, MLA_V, tile), lambda i: (0, i, 0, 0))],
        out_shape=[jax.ShapeDtypeStruct((MLA_HEADS, rows, MLA_QK), BF16),
                   jax.ShapeDtypeStruct((MLA_HEADS, rows // tile, MLA_V, tile), BF16)],
        compiler_params=_params("parallel"),
        name="kv_proj",
    )(ckv, kr, w_uk, w_uv_t)


def _attn_prompt_kernel(qt_ref, k_ref, vt_ref, o_ref, sa_sc, sb_sc, m_sc, l_sc, acc_sc):
    i = pl.program_id(1)
    m_sc[...] = jnp.full(m_sc.shape, -jnp.inf, F32)
    l_sc[...] = jnp.zeros(l_sc.shape, F32)
    acc_sc[...] = jnp.zeros(acc_sc.shape, F32)

    def scores(j, s_sc):
        start = pl.multiple_of(j * ATT_T, ATT_T)
        for hb in range(ATT_HB):
            s_sc[hb] = _dot(k_ref[hb, pl.ds(start, ATT_T), :], qt_ref[hb])

    def step(j, s_sc, masked):
        for hb in range(ATT_HB):
            s = s_sc[hb]
            if masked:
                kc = lax.broadcasted_iota(jnp.int32, s.shape, 0) // CHUNK
                qc = lax.broadcasted_iota(jnp.int32, s.shape, 1) // CHUNK
                s = jnp.where(kc <= qc, s, -jnp.inf)
            m_prev = m_sc[hb]
            m_new = jnp.maximum(m_prev, jnp.max(s, axis=0, keepdims=True))
            alpha = jnp.exp2(m_prev - m_new)
            p = jnp.exp2(s - m_new)
            l_sc[hb] = alpha * l_sc[hb] + jnp.sum(p, axis=0, keepdims=True)
            acc_sc[hb] = alpha * acc_sc[hb] + _dot(vt_ref[hb, j], p.astype(BF16))
            m_sc[hb] = m_new

    def pair(jj, carry):
        j = 2 * jj
        scores(j + 1, sb_sc)
        step(j, sa_sc, False)
        scores(j + 2, sa_sc)
        step(j + 1, sb_sc, False)
        return carry

    scores(0, sa_sc)
    lax.fori_loop(0, i // 2, pair, 0)

    @pl.when(i % 2 == 0)
    def _():
        step(i, sa_sc, True)

    @pl.when(i % 2 == 1)
    def _():
        scores(i, sb_sc)
        step(i - 1, sa_sc, False)
        step(i, sb_sc, True)

    for hb in range(ATT_HB):
        out_t = acc_sc[hb] / l_sc[hb]
        o_ref[:, hb * MLA_V:(hb + 1) * MLA_V] = out_t.T.astype(o_ref.dtype)


def attn_prompt(qt, k, vt):
    nblk = SEQ // ATT_T
    return pl.pallas_call(
        _attn_prompt_kernel,
        grid=(MLA_HEADS // ATT_HB, nblk),
        in_specs=[pl.BlockSpec((ATT_HB, MLA_QK, ATT_T), lambda g, i: (g, 0, i)),
                  pl.BlockSpec((ATT_HB, SEQ, MLA_QK), lambda g, i: (g, 0, 0)),
                  pl.BlockSpec((ATT_HB, nblk, MLA_V, ATT_T), lambda g, i: (g, 0, 0, 0))],
        out_specs=pl.BlockSpec((ATT_T, ATT_HB * MLA_V), lambda g, i: (i, g)),
        out_shape=jax.ShapeDtypeStruct((SEQ, MLA_HEADS * MLA_V), BF16),
        scratch_shapes=[pltpu.VMEM((ATT_HB, ATT_T, ATT_T), F32), pltpu.VMEM((ATT_HB, ATT_T, ATT_T), F32),
                        pltpu.VMEM((ATT_HB, 1, ATT_T), F32), pltpu.VMEM((ATT_HB, 1, ATT_T), F32),
                        pltpu.VMEM((ATT_HB, MLA_V, ATT_T), F32)],
        compiler_params=_params("parallel", "arbitrary"),
        name="attn_prompt",
    )(qt, k, vt)


def _attn_sample_kernel(qt_ref, k_ref, vt_ref, o_ref):
    qt = qt_ref[0]
    shape = (SAMPLE_KEYS_PAD, N_SAMPLE)
    kpos = lax.broadcasted_iota(jnp.int32, shape, 0)
    qpos = PAST_LEN + lax.broadcasted_iota(jnp.int32, shape, 1) % DEC_SEQ
    kchunk = jnp.where(kpos < SAMPLE_KEYS, kpos // CHUNK, SAMPLE_KEYS_PAD)
    visible = kchunk <= qpos // CHUNK
    stream = lax.broadcasted_iota(jnp.int32, (MLA_V, N_SAMPLE), 1) // DEC_SEQ
    out_t = jnp.zeros((MLA_V, N_SAMPLE), F32)
    for b in range(DEC_BATCH):
        s = jnp.where(visible, _dot(k_ref[0, b], qt), -jnp.inf)
        p = jnp.exp2(s - jnp.max(s, axis=0, keepdims=True))
        o_b = _dot(vt_ref[0, b], p.astype(BF16)) / jnp.sum(p, axis=0, keepdims=True)
        out_t = jnp.where(stream == b, o_b, out_t)
    o_ref[...] = out_t.T.astype(o_ref.dtype)


def attn_sample(qt, k, vt):
    return pl.pallas_call(
        _attn_sample_kernel,
        grid=(MLA_HEADS,),
        in_specs=[pl.BlockSpec((1, MLA_QK, N_SAMPLE), lambda h: (h, 0, SEQ // N_SAMPLE)),
                  pl.BlockSpec((1, DEC_BATCH, SAMPLE_KEYS_PAD, MLA_QK), lambda h: (h, 0, 0, 0)),
                  pl.BlockSpec((1, DEC_BATCH, MLA_V, SAMPLE_KEYS_PAD), lambda h: (h, 0, 0, 0))],
        out_specs=pl.BlockSpec((N_SAMPLE, MLA_V), lambda h: (0, h)),
        out_shape=jax.ShapeDtypeStruct((N_SAMPLE, MLA_HEADS * MLA_V), BF16),
        compiler_params=_params("parallel"),
        name="attn_sample",
    )(qt, k, vt)


CONV_PAD = 32
CONV_OFF = CONV_PAD - CONV_HIST


def _conv_kernel(za_ref, zb_ref, past_ref, w_ref, b_ref, g_ref, beta_ref, o_ref, hist_ref, ext_sc, *, t_rows):
    @pl.when(pl.program_id(1) == 0)
    def _():
        ext_sc[0:SUBLANES, :] = jnp.zeros((SUBLANES, CONV_CH), F32)
        ext_sc[CONV_OFF:CONV_PAD, :] = past_ref[0]

    u = za_ref[...] * _sigmoid(zb_ref[...])
    ext_sc[CONV_PAD:CONV_PAD + t_rows, :] = u
    cols = []
    for c0 in range(0, CONV_CH, LANES):
        acc = None
        for shift in range(SUBLANES):
            win = t_rows + (SUBLANES if shift else 0)
            group = None
            for base in range(0, CONV_PAD + SUBLANES, SUBLANES):
                tap = base + shift - CONV_OFF
                if 0 <= tap < CONV_WIDTH:
                    term = w_ref[tap:tap + 1, c0:c0 + LANES] * ext_sc[base:base + win, c0:c0 + LANES]
                    group = term if group is None else group + term
            part = group[shift:shift + t_rows]
            acc = part if acc is None else acc + part
        cols.append(acc)
    y = jnp.concatenate(cols, axis=1) + b_ref[...]
    yc = y - jnp.mean(y, axis=-1, keepdims=True)
    y = yc * lax.rsqrt(jnp.mean(yc * yc, axis=-1, keepdims=True) + EPS) * g_ref[...] + beta_ref[...]
    o_ref[...] = (y * _sigmoid(y)).astype(o_ref.dtype)
    tail = ext_sc[t_rows + CONV_OFF:t_rows + CONV_PAD, :]
    hist_ref[0] = tail
    ext_sc[CONV_OFF:CONV_PAD, :] = tail


def conv_mixer(z, past, w, b, g, beta, *, batch, seq, row_off, t_rows):
    steps = seq // t_rows
    first = row_off // t_rows

    def rows(bi, t):
        return first + bi * steps + t

    vec = pl.BlockSpec((1, CONV_CH), lambda bi, t: (0, 0))
    return pl.pallas_call(
        functools.partial(_conv_kernel, t_rows=t_rows),
        grid=(batch, steps),
        in_specs=[pl.BlockSpec((t_rows, CONV_CH), lambda bi, t: (rows(bi, t), Z_CA // CONV_CH)),
                  pl.BlockSpec((t_rows, CONV_CH), lambda bi, t: (rows(bi, t), Z_CB // CONV_CH)),
                  pl.BlockSpec((1, CONV_HIST, CONV_CH), lambda bi, t: (bi, 0, 0)),
                  pl.BlockSpec((CONV_WIDTH, CONV_CH), lambda bi, t: (0, 0)),
                  vec, vec, vec],
        out_specs=[pl.BlockSpec((t_rows, CONV_CH), lambda bi, t: (bi * steps + t, 0)),
                   pl.BlockSpec((1, CONV_HIST, CONV_CH), lambda bi, t: (bi, 0, 0))],
        out_shape=[jax.ShapeDtypeStruct((batch * seq, CONV_CH), BF16),
                   jax.ShapeDtypeStruct((batch, CONV_HIST, CONV_CH), F32)],
        scratch_shapes=[pltpu.VMEM((CONV_PAD + t_rows, CONV_CH), F32)],
        compiler_params=_params("parallel", "arbitrary"),
        name="conv_mixer",
    )(z, z, past, w, b.reshape(1, -1), g.reshape(1, -1), beta.reshape(1, -1))


def _pad_rows(x, rows):
    if x.shape[0] == rows:
        return x
    return jnp.concatenate([x, jnp.zeros((rows - x.shape[0], x.shape[1]), x.dtype)], axis=0)


def _log_sigmoid(x):
    return jnp.minimum(x, 0.0) - jnp.log1p(jnp.exp(-jnp.abs(x)))


def _gla_kernel(q_ref, k_ref, v_ref, r_ref, al_ref, wa_ref, ba_ref, gn_ref, s0_ref, o_ref, s_out_ref, s_sc,
                *, c, cps):
    step = pl.program_id(1)

    @pl.when(step == 0)
    def _():
        s_sc[...] = s0_ref[0]

    row = lax.broadcasted_iota(jnp.int32, (GLA_CP, GLA_CP), 0)
    col = lax.broadcasted_iota(jnp.int32, (GLA_CP, GLA_CP), 1)
    causal = col <= row
    tri = causal.astype(F32)
    for ci in range(cps):
        r0 = ci * c
        gate = _dot(al_ref[r0:r0 + c, :].astype(BF16), wa_ref[...]) + ba_ref[...]
        log_a = _pad_rows(_log_sigmoid(gate) / GLA_TAU, GLA_CP)
        bcum = jnp.dot(tri, log_a, precision=lax.Precision.HIGHEST, preferred_element_type=F32)
        for h in range(GLA_HEADS):
            ks = slice(h * GLA_DK, (h + 1) * GLA_DK)
            vs = slice(h * GLA_DV, (h + 1) * GLA_DV)
            b = bcum[:, ks]
            q = _pad_rows(q_ref[r0:r0 + c, ks], GLA_CP) * GLA_DK ** -0.5
            k = _pad_rows(k_ref[r0:r0 + c, ks], GLA_CP)
            v = _pad_rows(v_ref[r0:r0 + c, vs], GLA_CP).astype(BF16)
            q_t = (q * jnp.exp(b)).astype(BF16)
            k_t = (k * jnp.exp(-b)).astype(BF16)
            a = jnp.where(causal, _dot_nt(q_t, k_t), 0.0)
            s_prev = s_sc[h]
            o = _dot(q_t, s_prev.astype(BF16)) + _dot(a.astype(BF16), v)
            b_last = b[c - 1:c, :]
            k_dec = k * jnp.exp(b_last - b)
            k_dec_t = _pad_rows(k_dec, XPOSE).T
            decay = jnp.exp(_pad_rows(b, XPOSE).T[:, c - 1:c])
            s_sc[h] = decay * s_prev + _dot(k_dec_t.astype(BF16), _pad_rows(v, XPOSE))
            on = _rms(o[:c], gn_ref[...])
            r = r_ref[r0:r0 + c, vs]
            o_ref[r0:r0 + c, vs] = (on * (r * _sigmoid(r))).astype(o_ref.dtype)

    @pl.when(step == pl.num_programs(1) - 1)
    def _():
        s_out_ref[0] = s_sc[...]


def gla_mixer(z, s0, wa2, ba, gn, *, batch, seq, row_off, c, cps):
    t_rows = c * cps
    steps = seq // t_rows
    first = row_off // t_rows

    def rows(bi, t):
        return first + bi * steps + t

    state = pl.BlockSpec((1, GLA_HEADS, GLA_DK, GLA_DV), lambda bi, t: (bi, 0, 0, 0))
    return pl.pallas_call(
        functools.partial(_gla_kernel, c=c, cps=cps),
        grid=(batch, steps),
        in_specs=[pl.BlockSpec((t_rows, HK), lambda bi, t: (rows(bi, t), Z_GQ // HK)),
                  pl.BlockSpec((t_rows, HK), lambda bi, t: (rows(bi, t), Z_GK // HK)),
                  pl.BlockSpec((t_rows, HV), lambda bi, t: (rows(bi, t), Z_GV // HV)),
                  pl.BlockSpec((t_rows, HV), lambda bi, t: (rows(bi, t), Z_GR // HV)),
                  pl.BlockSpec((t_rows, LANES), lambda bi, t: (rows(bi, t), Z_AL // LANES)),
                  pl.BlockSpec((LANES, HK), lambda bi, t: (0, 0)),
                  pl.BlockSpec((1, HK), lambda bi, t: (0, 0)),
                  pl.BlockSpec((1, GLA_DV), lambda bi, t: (0, 0)),
                  state],
        out_specs=[pl.BlockSpec((t_rows, HV), lambda bi, t: (bi * steps + t, 0)), state],
        out_shape=[jax.ShapeDtypeStruct((batch * seq, HV), BF16),
                   jax.ShapeDtypeStruct((batch, GLA_HEADS, GLA_DK, GLA_DV), F32)],
        scratch_shapes=[pltpu.VMEM((GLA_HEADS, GLA_DK, GLA_DV), F32)],
        compiler_params=_params("parallel", "arbitrary"),
        name="gla_mixer",
    )(z, z, z, z, z, wa2, ba.reshape(1, -1), gn.reshape(1, -1), s0)


_SRC_CONV = MLA_Q_RANK + MLA_KV_RANK + MLA_ROPE
_SRC_GLA = _SRC_CONV + 2 * CONV_CH
D_IN = _SRC_GLA + 2 * HK + 2 * HV + GLA_GATE_RANK
_W_IN_PIECES = (
    (_SRC_CONV, CONV_CH, Z_CA), (_SRC_CONV + CONV_CH, CONV_CH, Z_CB),
    (_SRC_GLA + 2 * HK, HV, Z_GV), (_SRC_GLA + 2 * HK + HV, HV, Z_GR),
    (0, MLA_Q_RANK, Z_CQ),
    (_SRC_GLA, HK, Z_GQ), (_SRC_GLA + HK, HK, Z_GK),
    (MLA_Q_RANK, MLA_KV_RANK, Z_CKV),
    (MLA_Q_RANK + MLA_KV_RANK, MLA_ROPE, Z_KR),
    (_SRC_GLA + 2 * HK + 2 * HV, GLA_GATE_RANK, Z_AL),
)
PACK_ROWS = 256


def _pack_w_in_kernel(w_ref, o_ref):
    for src, width, dst in _W_IN_PIECES:
        lo = src // LANES * LANES
        hi = min(-(-(src + width) // LANES) * LANES, D_IN)
        piece = w_ref[:, lo:hi][:, src - lo:src - lo + width]
        pad = -width % LANES
        if pad:
            piece = jnp.concatenate([piece, jnp.zeros((piece.shape[0], pad), F32)], axis=1)
        o_ref[:, dst:dst + width + pad] = piece.astype(BF16)


def pack_w_in(w_in):
    return pl.pallas_call(
        _pack_w_in_kernel,
        grid=(DEPTH, D_MODEL // PACK_ROWS),
        in_specs=[pl.BlockSpec((None, PACK_ROWS, D_IN), lambda l, r: (l, r, 0))],
        out_specs=pl.BlockSpec((None, PACK_ROWS, Z_W), lambda l, r: (l, r, 0)),
        out_shape=jax.ShapeDtypeStruct((DEPTH, D_MODEL, Z_W), BF16),
        compiler_params=_params("parallel", "parallel"),
        name="pack_w_in",
    )(w_in)


def _rope_tables():
    half = MLA_ROPE // 2
    pos = jnp.concatenate([jnp.arange(SEQ, dtype=jnp.int32),
                           jnp.tile(PAST_LEN + jnp.arange(DEC_SEQ, dtype=jnp.int32), DEC_BATCH)])
    inv_freq = ROPE_THETA ** (-jnp.arange(half, dtype=F32) / half)
    ang = pos.astype(F32)[:, None] * inv_freq[None, :]
    cos, sin = jnp.cos(ang), jnp.sin(ang)
    c64 = jnp.concatenate([cos, cos], axis=1)
    s64 = jnp.concatenate([-sin, sin], axis=1)
    zero = jnp.zeros_like(s64)
    key = (jnp.concatenate([c64, zero], axis=1), jnp.concatenate([s64, zero], axis=1))
    return (cos.T, sin.T), key


@jax.jit
def _forward(x_prompt, x_sample, cache_ckv, cache_krope, cache_conv, state_gla, p_prompt, p_sample, norm_mix,
             w_in, mla_q_norm, mla_kv_norm, mla_w_uq, mla_w_uk, mla_w_uv, conv_dw_w, conv_dw_b, conv_ln_g,
             conv_ln_b, gla_w_a2, gla_b_a, gla_norm, w_out, norm_ffn, ffn_w_gate, ffn_w_up, ffn_w_down,
             norm_ple, ple_w_gate, ple_w_proj, norm_final):
    w_in_p = pack_w_in(w_in)
    w_uq_t = mla_w_uq.astype(BF16).transpose(0, 2, 1)
    w_uk = mla_w_uk.reshape(DEPTH, MLA_KV_RANK, MLA_HEADS * MLA_NOPE).astype(BF16)
    w_uv_t = mla_w_uv.reshape(DEPTH, MLA_KV_RANK, MLA_HEADS * MLA_V).astype(BF16).transpose(0, 2, 1)
    w_a2 = jnp.concatenate([gla_w_a2.astype(BF16),
                            jnp.zeros((DEPTH, LANES - GLA_GATE_RANK, HK), BF16)], axis=1)
    w_out_b = w_out.astype(BF16)
    w_gate, w_up, w_down = ffn_w_gate.astype(BF16), ffn_w_up.astype(BF16), ffn_w_down.astype(BF16)
    w_pg, w_pp = ple_w_gate.astype(BF16), ple_w_proj.astype(BF16)
    (cos_t, sin_t), (ck, sk) = _rope_tables()

    x = jnp.concatenate([x_prompt.reshape(SEQ, D_MODEL), x_sample.reshape(N_SAMPLE, D_MODEL)], axis=0)
    p_all = jnp.concatenate([p_prompt.reshape(DEPTH, SEQ, D_PLE), p_sample.reshape(DEPTH, N_SAMPLE, D_PLE)], axis=1)
    zero_conv = jnp.zeros((1, CONV_HIST, CONV_CH), F32)
    zero_gla = jnp.zeros((1, GLA_HEADS, GLA_DK, GLA_DV), F32)

    ckvs, krs, convs_p, convs_s, glas_p, glas_s = [], [], [], [], [], []
    for i in range(DEPTH):
        h = rmsnorm_rows(x, norm_mix[i], BF16)
        z = in_proj(h, w_in_p, i)

        qt = mla_q(z, mla_q_norm[i], w_uq_t[i], cos_t, sin_t)
        ckv, kr = mla_latent(z, mla_kv_norm[i], ck, sk)
        k_p, vt_p = kv_proj(ckv[:SEQ], kr[:SEQ], w_uk[i], w_uv_t[i], ATT_T)
        pad = SAMPLE_KEYS_PAD - SAMPLE_KEYS
        ckv_s = jnp.concatenate([cache_ckv[i], ckv[SEQ:].reshape(DEC_BATCH, DEC_SEQ, MLA_KV_RANK),
                                 jnp.zeros((DEC_BATCH, pad, MLA_KV_RANK), F32)], axis=1)
        kr_s = jnp.concatenate([cache_krope[i], kr[SEQ:].reshape(DEC_BATCH, DEC_SEQ, MLA_ROPE),
                                jnp.zeros((DEC_BATCH, pad, MLA_ROPE), F32)], axis=1)
        k_s, vt_s = kv_proj(ckv_s.reshape(-1, MLA_KV_RANK), kr_s.reshape(-1, MLA_ROPE), w_uk[i], w_uv_t[i],
                            SAMPLE_KEYS_PAD)
        k_s = k_s.reshape(MLA_HEADS, DEC_BATCH, SAMPLE_KEYS_PAD, MLA_QK)
        o_a = jnp.concatenate([attn_prompt(qt, k_p, vt_p), attn_sample(qt, k_s, vt_s)], axis=0)

        conv_args = (conv_dw_w[i], conv_dw_b[i], conv_ln_g[i], conv_ln_b[i])
        ob_p, conv_p = conv_mixer(z, zero_conv, *conv_args, batch=1, seq=SEQ, row_off=0, t_rows=CONV_T)
        ob_s, conv_s = conv_mixer(z, cache_conv[i], *conv_args, batch=DEC_BATCH, seq=DEC_SEQ, row_off=SEQ,
                                  t_rows=DEC_SEQ)
        o_b = jnp.concatenate([ob_p, ob_s], axis=0)

        gla_args = (w_a2[i], gla_b_a[i], gla_norm[i])
        oc_p, gla_p = gla_mixer(z, zero_gla, *gla_args, batch=1, seq=SEQ, row_off=0, c=CHUNK, cps=GLA_CPS)
        oc_s, gla_s = gla_mixer(z, state_gla[i], *gla_args, batch=DEC_BATCH, seq=DEC_SEQ, row_off=SEQ,
                                c=DEC_SEQ, cps=1)
        o_c = jnp.concatenate([oc_p, oc_s], axis=0)

        x = out_proj(o_a, o_b, o_c, w_out_b, x, i)
        h = rmsnorm_rows(x, norm_ffn[i], BF16)
        x = ffn_down(ffn_gate_up(h, w_gate, w_up, i), w_down, x, i)
        h = rmsnorm_rows(x, norm_ple[i], BF16)
        x = ple_update(h, p_all, w_pg, w_pp, x, i)

        ckvs.append(ckv)
        krs.append(kr)
        convs_p.append(conv_p)
        convs_s.append(conv_s)
        glas_p.append(gla_p)
        glas_s.append(gla_s)

    y = rmsnorm_rows(x, norm_final, F32)
    ckv_all, kr_all = jnp.stack(ckvs), jnp.stack(krs)
    return (y[:SEQ].reshape(1, SEQ, D_MODEL),
            y[SEQ:].reshape(DEC_BATCH, DEC_SEQ, D_MODEL),
            ckv_all[:, :SEQ].reshape(DEPTH, 1, SEQ, MLA_KV_RANK),
            kr_all[:, :SEQ].reshape(DEPTH, 1, SEQ, MLA_ROPE),
            jnp.stack(convs_p),
            jnp.stack(glas_p),
            ckv_all[:, SEQ:].reshape(DEPTH, DEC_BATCH, DEC_SEQ, MLA_KV_RANK),
            kr_all[:, SEQ:].reshape(DEPTH, DEC_BATCH, DEC_SEQ, MLA_ROPE),
            jnp.stack(convs_s),
            jnp.stack(glas_s))


def kernel(x_prompt, x_sample, cache_ckv, cache_krope, cache_conv, state_gla, p_prompt, p_sample, norm_mix, w_in, mla_q_norm, mla_kv_norm, mla_w_uq, mla_w_uk, mla_w_uv, conv_dw_w, conv_dw_b, conv_ln_g, conv_ln_b, gla_w_a2, gla_b_a, gla_norm, w_out, norm_ffn, ffn_w_gate, ffn_w_up, ffn_w_down, norm_ple, ple_w_gate, ple_w_proj, norm_final):
    return _forward(x_prompt, x_sample, cache_ckv, cache_krope, cache_conv, state_gla, p_prompt, p_sample,
                    norm_mix, w_in, mla_q_norm, mla_kv_norm, mla_w_uq, mla_w_uk, mla_w_uv, conv_dw_w, conv_dw_b,
                    conv_ln_g, conv_ln_b, gla_w_a2, gla_b_a, gla_norm, w_out, norm_ffn, ffn_w_gate, ffn_w_up,
                    ffn_w_down, norm_ple, ple_w_gate, ple_w_proj, norm_final)
```

```python
import functools

import jax
import jax.numpy as jnp
from jax import lax
from jax.experimental import pallas as pl
from jax.experimental.pallas import tpu as pltpu

F32 = jnp.float32
BF16 = jnp.bfloat16

D_MODEL = 4096
SEQ = 8192
DEPTH = 4
DEC_BATCH = 8
DEC_SEQ = 16
PAST_LEN = 1024
CHUNK = 64
EPS = 1e-6
D_PLE = 256
D_FF = 11008
MLA_HEADS = 16
MLA_Q_RANK = 1024
MLA_KV_RANK = 512
MLA_NOPE = 128
MLA_ROPE = 64
MLA_V = 128
MLA_QK = MLA_NOPE + MLA_ROPE
ROPE_THETA = 10000.0
CONV_CH = 1024
CONV_WIDTH = 31
CONV_HIST = CONV_WIDTH - 1
GLA_HEADS = 4
GLA_DK = 128
GLA_DV = 256
GLA_GATE_RANK = 16
GLA_TAU = 16.0
HK = GLA_HEADS * GLA_DK
HV = GLA_HEADS * GLA_DV

N_SAMPLE = DEC_BATCH * DEC_SEQ
ROWS = SEQ + N_SAMPLE
SAMPLE_KEYS = PAST_LEN + DEC_SEQ
SAMPLE_KEYS_PAD = 1152

LANES = 128
SUBLANES = 8
VMEM_LIMIT = 52 * 1024 * 1024

Z_CA = 0
Z_CB = 1024
Z_GV = 2048
Z_GR = 3072
Z_CQ = 4096
Z_GQ = 5120
Z_GK = 5632
Z_CKV = 6144
Z_KR = 6656
Z_AL = 6784
Z_W = 6912

TM = 832
TR = 416
TY = 512
TN_IN = 768
TN_OUT = 512
TN_FF = 256
TK_DOWN = D_FF // 2
TN_DOWN = 512
TN_PLE = 512
TQ = 640
ATT_T = 512
ATT_HB = 2
Q_SCALE = MLA_QK ** -0.5 * 1.4426950408889634
CONV_T = 128
GLA_CPS = 4
GLA_CP = 64
XPOSE = 128


def _params(*sem):
    return pltpu.CompilerParams(dimension_semantics=sem, vmem_limit_bytes=VMEM_LIMIT)


def _rms(x, g):
    return x * lax.rsqrt(jnp.mean(x * x, axis=-1, keepdims=True) + EPS) * g


def _sigmoid(x):
    return 1.0 / (1.0 + jnp.exp(-x))


def _dot(a, b):
    return jnp.dot(a, b, preferred_element_type=F32)


def _dot_nt(a, b):
    return lax.dot_general(a, b, (((1,), (1,)), ((), ())), preferred_element_type=F32)


def _fold_lanes(sq):
    part = sq[:, :LANES]
    for c in range(LANES, sq.shape[1], LANES):
        part = part + sq[:, c:c + LANES]
    return part


def _emit_norm_inputs(x_new, g_ref, xg_ref, ss_ref, first):
    xg_ref[...] = (x_new * g_ref[...]).astype(BF16)
    part = _fold_lanes(x_new * x_new)

    @pl.when(first)
    def _():
        ss_ref[...] = part

    @pl.when(jnp.logical_not(first))
    def _():
        ss_ref[...] += part


def _row_scale(ss_ref):
    return lax.rsqrt(jnp.sum(ss_ref[...], axis=-1, keepdims=True) * (1.0 / D_MODEL) + EPS)


def _norm_specs(imap_tile, imap_row, tn):
    return (pl.BlockSpec((1, tn), lambda *a: (0, imap_tile(*a)[1])),
            [pl.BlockSpec((TM, tn), imap_tile), pl.BlockSpec((TM, LANES), imap_row)],
            [jax.ShapeDtypeStruct((ROWS, D_MODEL), BF16), jax.ShapeDtypeStruct((ROWS, LANES), F32)])


def _norm_prep_kernel(x_ref, g_ref, xg_ref, ss_ref):
    x = x_ref[...]
    xg_ref[...] = (x * g_ref[...]).astype(BF16)
    ss_ref[...] = _fold_lanes(x * x)


def norm_prep(x, g):
    rows, d = x.shape
    return pl.pallas_call(
        _norm_prep_kernel,
        grid=(rows // TR,),
        in_specs=[pl.BlockSpec((TR, d), lambda i: (i, 0)),
                  pl.BlockSpec((1, d), lambda i: (0, 0))],
        out_specs=[pl.BlockSpec((TR, d), lambda i: (i, 0)), pl.BlockSpec((TR, LANES), lambda i: (i, 0))],
        out_shape=[jax.ShapeDtypeStruct((rows, d), BF16), jax.ShapeDtypeStruct((rows, LANES), F32)],
        compiler_params=_params("parallel"),
        name="norm_prep",
    )(x, g.reshape(1, d))


def _rmsnorm_kernel(x_ref, g_ref, o_ref):
    o_ref[...] = _rms(x_ref[...], g_ref[...]).astype(o_ref.dtype)


def rmsnorm_rows(x, g, row_off, rows, tile):
    d = x.shape[1]
    first = row_off // tile
    return pl.pallas_call(
        _rmsnorm_kernel,
        grid=(rows // tile,),
        in_specs=[pl.BlockSpec((tile, d), lambda i: (first + i, 0)),
                  pl.BlockSpec((1, d), lambda i: (0, 0))],
        out_specs=pl.BlockSpec((tile, d), lambda i: (i, 0)),
        out_shape=jax.ShapeDtypeStruct((rows, d), F32),
        compiler_params=_params("parallel"),
        name="rmsnorm_rows",
    )(x, g.reshape(1, d))


def _in_proj_kernel(xg_ref, ss_ref, w_ref, o_ref):
    o_ref[...] = _dot(xg_ref[...], w_ref[...]) * _row_scale(ss_ref)


def in_proj(xg, ss, w, layer):
    rows, k = xg.shape
    n = w.shape[2]
    return pl.pallas_call(
        _in_proj_kernel,
        grid=(rows // TM, n // TN_IN),
        in_specs=[pl.BlockSpec((TM, k), lambda i, j: (i, 0)),
                  pl.BlockSpec((TM, LANES), lambda i, j: (i, 0)),
                  pl.BlockSpec((None, k, TN_IN), lambda i, j: (layer, 0, j))],
        out_specs=pl.BlockSpec((TM, TN_IN), lambda i, j: (i, j)),
        out_shape=jax.ShapeDtypeStruct((rows, n), F32),
        compiler_params=_params("parallel", "arbitrary"),
        name="in_proj",
    )(xg, ss, w)


def _out_proj_kernel(oa_ref, ob_ref, oc_ref, wa_ref, wb_ref, wc_ref, x_ref, g_ref, o_ref, xg_ref, ss_ref):
    acc = _dot(oa_ref[...], wa_ref[...])
    acc += _dot(ob_ref[...], wb_ref[...])
    acc += _dot(oc_ref[...], wc_ref[...])
    x_new = x_ref[...] + acc
    o_ref[...] = x_new
    _emit_norm_inputs(x_new, g_ref, xg_ref, ss_ref, pl.program_id(1) == 0)


def out_proj(o_a, o_b, o_c, w, x, g_next, layer):
    rows = x.shape[0]
    da, db, dc = o_a.shape[1], o_b.shape[1], o_c.shape[1]
    g_spec, n_specs, n_shapes = _norm_specs(lambda i, j: (i, j), lambda i, j: (i, 0), TN_OUT)
    return pl.pallas_call(
        _out_proj_kernel,
        grid=(rows // TM, D_MODEL // TN_OUT),
        in_specs=[pl.BlockSpec((TM, da), lambda i, j: (i, 0)),
                  pl.BlockSpec((TM, db), lambda i, j: (i, 0)),
                  pl.BlockSpec((TM, dc), lambda i, j: (i, 0)),
                  pl.BlockSpec((None, da, TN_OUT), lambda i, j: (layer, 0, j)),
                  pl.BlockSpec((None, db, TN_OUT), lambda i, j: (layer, da // db, j)),
                  pl.BlockSpec((None, dc, TN_OUT), lambda i, j: (layer, (da + db) // dc, j)),
                  pl.BlockSpec((TM, TN_OUT), lambda i, j: (i, j)),
                  g_spec],
        out_specs=[pl.BlockSpec((TM, TN_OUT), lambda i, j: (i, j))] + n_specs,
        out_shape=[jax.ShapeDtypeStruct((rows, D_MODEL), F32)] + n_shapes,
        compiler_params=_params("parallel", "arbitrary"),
        name="out_proj",
    )(o_a, o_b, o_c, w, w, w, x, g_next.reshape(1, -1))


def _ffn_gate_up_kernel(xg_ref, ss_ref, wg_ref, wu_ref, o_ref):
    xg = xg_ref[...]
    r = _row_scale(ss_ref)
    g = _dot(xg, wg_ref[...]) * r
    u = _dot(xg, wu_ref[...]) * r
    o_ref[...] = (g * _sigmoid(g) * u).astype(o_ref.dtype)


def ffn_gate_up(xg, ss, wg, wu, layer):
    rows, k = xg.shape
    return pl.pallas_call(
        _ffn_gate_up_kernel,
        grid=(rows // TM, D_FF // TN_FF),
        in_specs=[pl.BlockSpec((TM, k), lambda i, j: (i, 0)),
                  pl.BlockSpec((TM, LANES), lambda i, j: (i, 0)),
                  pl.BlockSpec((None, k, TN_FF), lambda i, j: (layer, 0, j)),
                  pl.BlockSpec((None, k, TN_FF), lambda i, j: (layer, 0, j))],
        out_specs=pl.BlockSpec((TM, TN_FF), lambda i, j: (i, j)),
        out_shape=jax.ShapeDtypeStruct((rows, D_FF), BF16),
        compiler_params=_params("parallel", "arbitrary"),
        name="ffn_gate_up",
    )(xg, ss, wg, wu)


def _ffn_down_kernel(a_ref, w_ref, x_ref, g_ref, o_ref, xg_ref, ss_ref):
    part = _dot(a_ref[...], w_ref[...])
    k = pl.program_id(2)

    @pl.when(k == 0)
    def _():
        o_ref[...] = x_ref[...] + part

    @pl.when(k != 0)
    def _():
        o_ref[...] += part

    @pl.when(k == pl.num_programs(2) - 1)
    def _():
        _emit_norm_inputs(o_ref[...], g_ref, xg_ref, ss_ref, pl.program_id(1) == 0)


def ffn_down(a, w, x, g_next, layer):
    rows = x.shape[0]
    g_spec, n_specs, n_shapes = _norm_specs(lambda i, j, k: (i, j), lambda i, j, k: (i, 0), TN_DOWN)
    return pl.pallas_call(
        _ffn_down_kernel,
        grid=(rows // TM, D_MODEL // TN_DOWN, D_FF // TK_DOWN),
        in_specs=[pl.BlockSpec((TM, TK_DOWN), lambda i, j, k: (i, k)),
                  pl.BlockSpec((None, TK_DOWN, TN_DOWN), lambda i, j, k: (layer, k, j)),
                  pl.BlockSpec((TM, TN_DOWN), lambda i, j, k: (i, j)),
                  g_spec],
        out_specs=[pl.BlockSpec((TM, TN_DOWN), lambda i, j, k: (i, j))] + n_specs,
        out_shape=[jax.ShapeDtypeStruct((rows, D_MODEL), F32)] + n_shapes,
        compiler_params=_params("parallel", "arbitrary", "arbitrary"),
        name="ffn_down",
    )(a, w, x, g_next.reshape(1, -1))


def _ple_kernel(xg_ref, ss_ref, p_ref, wg_ref, wp_ref, x_ref, *rest, emit):
    gate = _sigmoid(_dot(xg_ref[...], wg_ref[...]) * _row_scale(ss_ref))
    proj = _dot(p_ref[...].astype(BF16), wp_ref[...])
    x_new = x_ref[...] + gate * proj
    if emit:
        g_ref, o_ref, xg_out_ref, ss_out_ref = rest
        _emit_norm_inputs(x_new, g_ref, xg_out_ref, ss_out_ref, pl.program_id(1) == 0)
    else:
        (o_ref,) = rest
    o_ref[...] = x_new


def ple_update(xg, ss, p, wg, wp, x, g_next, layer):
    rows, k = xg.shape
    emit = g_next is not None
    g_spec, n_specs, n_shapes = _norm_specs(lambda i, j: (i, j), lambda i, j: (i, 0), TN_PLE)
    outs = pl.pallas_call(
        functools.partial(_ple_kernel, emit=emit),
        grid=(rows // TM, D_MODEL // TN_PLE),
        in_specs=[pl.BlockSpec((TM, k), lambda i, j: (i, 0)),
                  pl.BlockSpec((TM, LANES), lambda i, j: (i, 0)),
                  pl.BlockSpec((None, TM, D_PLE), lambda i, j: (layer, i, 0)),
                  pl.BlockSpec((None, k, TN_PLE), lambda i, j: (layer, 0, j)),
                  pl.BlockSpec((None, D_PLE, TN_PLE), lambda i, j: (layer, 0, j)),
                  pl.BlockSpec((TM, TN_PLE), lambda i, j: (i, j))] + ([g_spec] if emit else []),
        out_specs=[pl.BlockSpec((TM, TN_PLE), lambda i, j: (i, j))] + (n_specs if emit else []),
        out_shape=[jax.ShapeDtypeStruct((rows, D_MODEL), F32)] + (n_shapes if emit else []),
        compiler_params=_params("parallel", "arbitrary"),
        name="ple_update",
    )(xg, ss, p, wg, wp, x, *([g_next.reshape(1, -1)] if emit else []))
    return outs if emit else (outs[0], None, None)


def _swap_rope_halves(x):
    lane = lax.broadcasted_iota(jnp.int32, x.shape, 1)
    first = (lane % MLA_ROPE) < (MLA_ROPE // 2)
    return jnp.where(first, pltpu.roll(x, LANES - MLA_ROPE // 2, 1), pltpu.roll(x, MLA_ROPE // 2, 1))


def _mla_q_kernel(z_ref, g_ref, wt_ref, cos_ref, sin_ref, qt_ref):
    cq = _rms(z_ref[...], g_ref[...]).astype(BF16)
    qt = _dot_nt(wt_ref[...], cq)
    cos, sin = cos_ref[...], sin_ref[...]
    half = MLA_ROPE // 2
    for h in range(MLA_HEADS):
        base = h * MLA_QK
        x1 = qt[base + MLA_NOPE:base + MLA_NOPE + half]
        x2 = qt[base + MLA_NOPE + half:base + MLA_QK]
        qt_ref[h, :MLA_NOPE, :] = (qt[base:base + MLA_NOPE] * Q_SCALE).astype(BF16)
        qt_ref[h, MLA_NOPE:MLA_NOPE + half, :] = ((x1 * cos - x2 * sin) * Q_SCALE).astype(BF16)
        qt_ref[h, MLA_NOPE + half:, :] = ((x2 * cos + x1 * sin) * Q_SCALE).astype(BF16)


def mla_q(z, g, w_uq_t, cos_t, sin_t):
    rows = z.shape[0]
    tab_spec = pl.BlockSpec((MLA_ROPE // 2, TQ), lambda i: (0, i))
    return pl.pallas_call(
        _mla_q_kernel,
        grid=(rows // TQ,),
        in_specs=[pl.BlockSpec((TQ, MLA_Q_RANK), lambda i: (i, Z_CQ // MLA_Q_RANK)),
                  pl.BlockSpec((1, MLA_Q_RANK), lambda i: (0, 0)),
                  pl.BlockSpec(w_uq_t.shape, lambda i: (0, 0)),
                  tab_spec, tab_spec],
        out_specs=pl.BlockSpec((MLA_HEADS, MLA_QK, TQ), lambda i: (0, 0, i)),
        out_shape=jax.ShapeDtypeStruct((MLA_HEADS, MLA_QK, rows), BF16),
        compiler_params=_params("parallel"),
        name="mla_q",
    )(z, g.reshape(1, -1), w_uq_t, cos_t, sin_t)


def _mla_latent_kernel(zc_ref, zk_ref, g_ref, ck_ref, sk_ref, ckv_ref, kr_ref):
    ckv_ref[...] = _rms(zc_ref[...], g_ref[...])
    kr = zk_ref[...]
    rot = kr * ck_ref[...] + _swap_rope_halves(kr) * sk_ref[...]
    kr_ref[...] = rot[:, :MLA_ROPE]


def mla_latent(z, g, ck, sk):
    rows = z.shape[0]
    return pl.pallas_call(
        _mla_latent_kernel,
        grid=(rows // TR,),
        in_specs=[pl.BlockSpec((TR, MLA_KV_RANK), lambda i: (i, Z_CKV // MLA_KV_RANK)),
                  pl.BlockSpec((TR, LANES), lambda i: (i, Z_KR // LANES)),
                  pl.BlockSpec((1, MLA_KV_RANK), lambda i: (0, 0)),
                  pl.BlockSpec((TR, LANES), lambda i: (i, 0)),
                  pl.BlockSpec((TR, LANES), lambda i: (i, 0))],
        out_specs=[pl.BlockSpec((TR, MLA_KV_RANK), lambda i: (i, 0)),
                   pl.BlockSpec((TR, MLA_ROPE), lambda i: (i, 0))],
        out_shape=[jax.ShapeDtypeStruct((rows, MLA_KV_RANK), F32),
                   jax.ShapeDtypeStruct((rows, MLA_ROPE), F32)],
        compiler_params=_params("parallel"),
        name="mla_latent",
    )(z, z, g.reshape(1, -1), ck, sk)


def _kv_proj_kernel(ckv_ref, kr_ref, wk_ref, wvt_ref, k_ref, vt_ref):
    c = ckv_ref[...].astype(BF16)
    kn = _dot(c, wk_ref[...])
    vt = _dot_nt(wvt_ref[...], c)
    kr = kr_ref[...].astype(BF16)
    for h in range(MLA_HEADS):
        k_ref[h, :, :MLA_NOPE] = kn[:, h * MLA_NOPE:(h + 1) * MLA_NOPE].astype(BF16)
        k_ref[h, :, MLA_NOPE:] = kr
        vt_ref[h, 0] = vt[h * MLA_V:(h + 1) * MLA_V].astype(BF16)


def kv_proj(ckv, kr, w_uk, w_uv_t, tile, rows):
    return pl.pallas_call(
        _kv_proj_kernel,
        grid=(rows // tile,),
        in_specs=[pl.BlockSpec((tile, MLA_KV_RANK), lambda i: (i, 0)),
                  pl.BlockSpec((tile, MLA_ROPE), lambda i: (i, 0)),
                  pl.BlockSpec(w_uk.shape, lambda i: (0, 0)),
                  pl.BlockSpec(w_uv_t.shape, lambda i: (0, 0))],
        out_specs=[pl.BlockSpec((MLA_HEADS, tile, MLA_QK), lambda i: (0, i, 0)),
                   pl.BlockSpec((MLA_HEADS, 1, MLA_V, tile), lambda i: (0, i, 0, 0))],
        out_shape=[jax.ShapeDtypeStruct((MLA_HEADS, rows, MLA_QK), BF16),
                   jax.ShapeDtypeStruct((MLA_HEADS, rows // tile, MLA_V, tile), BF16)],
        compiler_params=_params("parallel"),
        name="kv_proj",
    )(ckv, kr, w_uk, w_uv_t)


def _attn_prompt_kernel(qt_ref, k_ref, vt_ref, o_ref, sa_sc, sb_sc, m_sc, l_sc, acc_sc):
    i = pl.program_id(1)
    m_sc[...] = jnp.full(m_sc.shape, -jnp.inf, F32)
    l_sc[...] = jnp.zeros(l_sc.shape, F32)
    acc_sc[...] = jnp.zeros(acc_sc.shape, F32)

    def scores(j, s_sc):
        start = pl.multiple_of(j * ATT_T, ATT_T)
        for hb in range(ATT_HB):
            s_sc[hb] = _dot(k_ref[hb, pl.ds(start, ATT_T), :], qt_ref[hb])

    def step(j, s_sc, masked):
        for hb in range(ATT_HB):
            s = s_sc[hb]
            if masked:
                kc = lax.broadcasted_iota(jnp.int32, s.shape, 0) // CHUNK
                qc = lax.broadcasted_iota(jnp.int32, s.shape, 1) // CHUNK
                s = jnp.where(kc <= qc, s, -jnp.inf)
            m_prev = m_sc[hb]
            m_new = jnp.maximum(m_prev, jnp.max(s, axis=0, keepdims=True))
            alpha = jnp.exp2(m_prev - m_new)
            p = jnp.exp2(s - m_new)
            l_sc[hb] = alpha * l_sc[hb] + jnp.sum(p, axis=0, keepdims=True)
            acc_sc[hb] = alpha * acc_sc[hb] + _dot(vt_ref[hb, j], p.astype(BF16))
            m_sc[hb] = m_new

    def pair(jj, carry):
        j = 2 * jj
        scores(j + 1, sb_sc)
        step(j, sa_sc, False)
        scores(j + 2, sa_sc)
        step(j + 1, sb_sc, False)
        return carry

    scores(0, sa_sc)
    lax.fori_loop(0, i // 2, pair, 0)

    @pl.when(i % 2 == 0)
    def _():
        step(i, sa_sc, True)

    @pl.when(i % 2 == 1)
    def _():
        scores(i, sb_sc)
        step(i - 1, sa_sc, False)
        step(i, sb_sc, True)

    for hb in range(ATT_HB):
        out_t = acc_sc[hb] / l_sc[hb]
        o_ref[:, hb * MLA_V:(hb + 1) * MLA_V] = out_t.T.astype(o_ref.dtype)


def attn_prompt(qt, k, vt):
    nblk = SEQ // ATT_T
    return pl.pallas_call(
        _attn_prompt_kernel,
        grid=(MLA_HEADS // ATT_HB, nblk),
        in_specs=[pl.BlockSpec((ATT_HB, MLA_QK, ATT_T), lambda g, i: (g, 0, i)),
                  pl.BlockSpec((ATT_HB, SEQ, MLA_QK), lambda g, i: (g, 0, 0)),
                  pl.BlockSpec((ATT_HB, nblk, MLA_V, ATT_T), lambda g, i: (g, 0, 0, 0))],
        out_specs=pl.BlockSpec((ATT_T, ATT_HB * MLA_V), lambda g, i: (i, g)),
        out_shape=jax.ShapeDtypeStruct((ROWS, MLA_HEADS * MLA_V), BF16),
        scratch_shapes=[pltpu.VMEM((ATT_HB, ATT_T, ATT_T), F32), pltpu.VMEM((ATT_HB, ATT_T, ATT_T), F32),
                        pltpu.VMEM((ATT_HB, 1, ATT_T), F32), pltpu.VMEM((ATT_HB, 1, ATT_T), F32),
                        pltpu.VMEM((ATT_HB, MLA_V, ATT_T), F32)],
        compiler_params=_params("parallel", "arbitrary"),
        name="attn_prompt",
    )(qt, k, vt)


def _attn_sample_kernel(qt_ref, k_ref, vt_ref, into_ref, o_ref):
    del into_ref
    qt = qt_ref[0]
    shape = (SAMPLE_KEYS_PAD, N_SAMPLE)
    kpos = lax.broadcasted_iota(jnp.int32, shape, 0)
    qpos = PAST_LEN + lax.broadcasted_iota(jnp.int32, shape, 1) % DEC_SEQ
    kchunk = jnp.where(kpos < SAMPLE_KEYS, kpos // CHUNK, SAMPLE_KEYS_PAD)
    visible = kchunk <= qpos // CHUNK
    stream = lax.broadcasted_iota(jnp.int32, (MLA_V, N_SAMPLE), 1) // DEC_SEQ
    out_t = jnp.zeros((MLA_V, N_SAMPLE), F32)
    for b in range(DEC_BATCH):
        s = jnp.where(visible, _dot(k_ref[0, b], qt), -jnp.inf)
        p = jnp.exp2(s - jnp.max(s, axis=0, keepdims=True))
        o_b = _dot(vt_ref[0, b], p.astype(BF16)) / jnp.sum(p, axis=0, keepdims=True)
        out_t = jnp.where(stream == b, o_b, out_t)
    o_ref[...] = out_t.T.astype(o_ref.dtype)


def attn_sample(qt, k, vt, into):
    return pl.pallas_call(
        _attn_sample_kernel,
        grid=(MLA_HEADS,),
        in_specs=[pl.BlockSpec((1, MLA_QK, N_SAMPLE), lambda h: (h, 0, SEQ // N_SAMPLE)),
                  pl.BlockSpec((1, DEC_BATCH, SAMPLE_KEYS_PAD, MLA_QK), lambda h: (h, 0, 0, 0)),
                  pl.BlockSpec((1, DEC_BATCH, MLA_V, SAMPLE_KEYS_PAD), lambda h: (h, 0, 0, 0)),
                  pl.BlockSpec(memory_space=pl.ANY)],
        out_specs=pl.BlockSpec((N_SAMPLE, MLA_V), lambda h: (SEQ // N_SAMPLE, h)),
        out_shape=jax.ShapeDtypeStruct(into.shape, into.dtype),
        input_output_aliases={3: 0},
        compiler_params=_params("parallel"),
        name="attn_sample",
    )(qt, k, vt, into)


CONV_PAD = 32
CONV_OFF = CONV_PAD - CONV_HIST


def _conv_kernel(za_ref, zb_ref, past_ref, w_ref, b_ref, g_ref, beta_ref, *rest, t_rows):
    o_ref, hist_ref, ext_sc = rest[-3:]

    @pl.when(pl.program_id(1) == 0)
    def _():
        ext_sc[0:SUBLANES, :] = jnp.zeros((SUBLANES, CONV_CH), F32)
        ext_sc[CONV_OFF:CONV_PAD, :] = past_ref[0]

    u = za_ref[...] * _sigmoid(zb_ref[...])
    ext_sc[CONV_PAD:CONV_PAD + t_rows, :] = u
    cols = []
    for c0 in range(0, CONV_CH, LANES):
        acc = None
        for shift in range(SUBLANES):
            win = t_rows + (SUBLANES if shift else 0)
            group = None
            for base in range(0, CONV_PAD + SUBLANES, SUBLANES):
                tap = base + shift - CONV_OFF
                if 0 <= tap < CONV_WIDTH:
                    term = w_ref[tap:tap + 1, c0:c0 + LANES] * ext_sc[base:base + win, c0:c0 + LANES]
                    group = term if group is None else group + term
            part = group[shift:shift + t_rows]
            acc = part if acc is None else acc + part
        cols.append(acc)
    y = jnp.concatenate(cols, axis=1) + b_ref[...]
    yc = y - jnp.mean(y, axis=-1, keepdims=True)
    y = yc * lax.rsqrt(jnp.mean(yc * yc, axis=-1, keepdims=True) + EPS) * g_ref[...] + beta_ref[...]
    o_ref[...] = (y * _sigmoid(y)).astype(o_ref.dtype)
    tail = ext_sc[t_rows + CONV_OFF:t_rows + CONV_PAD, :]
    hist_ref[0] = tail
    ext_sc[CONV_OFF:CONV_PAD, :] = tail


def _into_args(into):
    if into is None:
        return [], [], None
    return [pl.BlockSpec(memory_space=pl.ANY)], [into], into


def conv_mixer(z, past, w, b, g, beta, *, batch, seq, row_off, t_rows, into=None):
    steps = seq // t_rows
    first = row_off // t_rows

    def rows(bi, t):
        return first + bi * steps + t

    vec = pl.BlockSpec((1, CONV_CH), lambda bi, t: (0, 0))
    extra_specs, extra_args, alias = _into_args(into)
    return pl.pallas_call(
        functools.partial(_conv_kernel, t_rows=t_rows),
        grid=(batch, steps),
        in_specs=[pl.BlockSpec((t_rows, CONV_CH), lambda bi, t: (rows(bi, t), Z_CA // CONV_CH)),
                  pl.BlockSpec((t_rows, CONV_CH), lambda bi, t: (rows(bi, t), Z_CB // CONV_CH)),
                  pl.BlockSpec((1, CONV_HIST, CONV_CH), lambda bi, t: (bi, 0, 0)),
                  pl.BlockSpec((CONV_WIDTH, CONV_CH), lambda bi, t: (0, 0)),
                  vec, vec, vec] + extra_specs,
        out_specs=[pl.BlockSpec((t_rows, CONV_CH), lambda bi, t: (rows(bi, t), 0)),
                   pl.BlockSpec((1, CONV_HIST, CONV_CH), lambda bi, t: (bi, 0, 0))],
        out_shape=[jax.ShapeDtypeStruct((ROWS, CONV_CH), BF16),
                   jax.ShapeDtypeStruct((batch, CONV_HIST, CONV_CH), F32)],
        input_output_aliases={} if alias is None else {7: 0},
        scratch_shapes=[pltpu.VMEM((CONV_PAD + t_rows, CONV_CH), F32)],
        compiler_params=_params("parallel", "arbitrary"),
        name="conv_mixer",
    )(z, z, past, w, b.reshape(1, -1), g.reshape(1, -1), beta.reshape(1, -1), *extra_args)


def _pad_rows(x, rows):
    if x.shape[0] == rows:
        return x
    return jnp.concatenate([x, jnp.zeros((rows - x.shape[0], x.shape[1]), x.dtype)], axis=0)


def _log_sigmoid(x):
    return jnp.minimum(x, 0.0) - jnp.log1p(jnp.exp(-jnp.abs(x)))


def _gla_kernel(q_ref, k_ref, v_ref, r_ref, al_ref, wa_ref, ba_ref, gn_ref, s0_ref, *rest, c, cps):
    o_ref, s_out_ref, s_sc = rest[-3:]
    step = pl.program_id(1)

    @pl.when(step == 0)
    def _():
        s_sc[...] = s0_ref[0]

    row = lax.broadcasted_iota(jnp.int32, (GLA_CP, GLA_CP), 0)
    col = lax.broadcasted_iota(jnp.int32, (GLA_CP, GLA_CP), 1)
    causal = col <= row
    tri = causal.astype(F32)
    for ci in range(cps):
        r0 = ci * c
        gate = _dot(al_ref[r0:r0 + c, :].astype(BF16), wa_ref[...]) + ba_ref[...]
        log_a = _pad_rows(_log_sigmoid(gate) / GLA_TAU, GLA_CP)
        bcum = jnp.dot(tri, log_a, precision=lax.Precision.HIGHEST, preferred_element_type=F32)
        for h in range(GLA_HEADS):
            ks = slice(h * GLA_DK, (h + 1) * GLA_DK)
            vs = slice(h * GLA_DV, (h + 1) * GLA_DV)
            b = bcum[:, ks]
            q = _pad_rows(q_ref[r0:r0 + c, ks], GLA_CP) * GLA_DK ** -0.5
            k = _pad_rows(k_ref[r0:r0 + c, ks], GLA_CP)
            v = _pad_rows(v_ref[r0:r0 + c, vs], GLA_CP).astype(BF16)
            q_t = (q * jnp.exp(b)).astype(BF16)
            k_t = (k * jnp.exp(-b)).astype(BF16)
            a = jnp.where(causal, _dot_nt(q_t, k_t), 0.0)
            s_prev = s_sc[h]
            o = _dot(q_t, s_prev.astype(BF16)) + _dot(a.astype(BF16), v)
            b_last = b[c - 1:c, :]
            k_dec = k * jnp.exp(b_last - b)
            k_dec_t = _pad_rows(k_dec, XPOSE).T
            decay = jnp.exp(_pad_rows(b, XPOSE).T[:, c - 1:c])
            s_sc[h] = decay * s_prev + _dot(k_dec_t.astype(BF16), _pad_rows(v, XPOSE))
            on = _rms(o[:c], gn_ref[...])
            r = r_ref[r0:r0 + c, vs]
            o_ref[r0:r0 + c, vs] = (on * (r * _sigmoid(r))).astype(o_ref.dtype)

    @pl.when(step == pl.num_programs(1) - 1)
    def _():
        s_out_ref[0] = s_sc[...]


def gla_mixer(z, s0, wa2, ba, gn, *, batch, seq, row_off, c, cps, into=None):
    t_rows = c * cps
    steps = seq // t_rows
    first = row_off // t_rows

    def rows(bi, t):
        return first + bi * steps + t

    state = pl.BlockSpec((1, GLA_HEADS, GLA_DK, GLA_DV), lambda bi, t: (bi, 0, 0, 0))
    extra_specs, extra_args, alias = _into_args(into)
    return pl.pallas_call(
        functools.partial(_gla_kernel, c=c, cps=cps),
        grid=(batch, steps),
        in_specs=[pl.BlockSpec((t_rows, HK), lambda bi, t: (rows(bi, t), Z_GQ // HK)),
                  pl.BlockSpec((t_rows, HK), lambda bi, t: (rows(bi, t), Z_GK // HK)),
                  pl.BlockSpec((t_rows, HV), lambda bi, t: (rows(bi, t), Z_GV // HV)),
                  pl.BlockSpec((t_rows, HV), lambda bi, t: (rows(bi, t), Z_GR // HV)),
                  pl.BlockSpec((t_rows, LANES), lambda bi, t: (rows(bi, t), Z_AL // LANES)),
                  pl.BlockSpec((LANES, HK), lambda bi, t: (0, 0)),
                  pl.BlockSpec((1, HK), lambda bi, t: (0, 0)),
                  pl.BlockSpec((1, GLA_DV), lambda bi, t: (0, 0)),
                  state] + extra_specs,
        out_specs=[pl.BlockSpec((t_rows, HV), lambda bi, t: (rows(bi, t), 0)), state],
        out_shape=[jax.ShapeDtypeStruct((ROWS, HV), BF16),
                   jax.ShapeDtypeStruct((batch, GLA_HEADS, GLA_DK, GLA_DV), F32)],
        input_output_aliases={} if alias is None else {9: 0},
        scratch_shapes=[pltpu.VMEM((GLA_HEADS, GLA_DK, GLA_DV), F32)],
        compiler_params=_params("parallel", "arbitrary"),
        name="gla_mixer",
    )(z, z, z, z, z, wa2, ba.reshape(1, -1), gn.reshape(1, -1), s0, *extra_args)


_SRC_CONV = MLA_Q_RANK + MLA_KV_RANK + MLA_ROPE
_SRC_GLA = _SRC_CONV + 2 * CONV_CH
D_IN = _SRC_GLA + 2 * HK + 2 * HV + GLA_GATE_RANK
_W_IN_PIECES = (
    (_SRC_CONV, CONV_CH, Z_CA), (_SRC_CONV + CONV_CH, CONV_CH, Z_CB),
    (_SRC_GLA + 2 * HK, HV, Z_GV), (_SRC_GLA + 2 * HK + HV, HV, Z_GR),
    (0, MLA_Q_RANK, Z_CQ),
    (_SRC_GLA, HK, Z_GQ), (_SRC_GLA + HK, HK, Z_GK),
    (MLA_Q_RANK, MLA_KV_RANK, Z_CKV),
    (MLA_Q_RANK + MLA_KV_RANK, MLA_ROPE, Z_KR),
    (_SRC_GLA + 2 * HK + 2 * HV, GLA_GATE_RANK, Z_AL),
)
PACK_ROWS = 256


def _pack_w_in_kernel(w_ref, o_ref):
    for src, width, dst in _W_IN_PIECES:
        lo = src // LANES * LANES
        hi = min(-(-(src + width) // LANES) * LANES, D_IN)
        piece = w_ref[:, lo:hi][:, src - lo:src - lo + width]
        pad = -width % LANES
        if pad:
            piece = jnp.concatenate([piece, jnp.zeros((piece.shape[0], pad), F32)], axis=1)
        o_ref[:, dst:dst + width + pad] = piece.astype(BF16)


def pack_w_in(w_in):
    return pl.pallas_call(
        _pack_w_in_kernel,
        grid=(DEPTH, D_MODEL // PACK_ROWS),
        in_specs=[pl.BlockSpec((None, PACK_ROWS, D_IN), lambda l, r: (l, r, 0))],
        out_specs=pl.BlockSpec((None, PACK_ROWS, Z_W), lambda l, r: (l, r, 0)),
        out_shape=jax.ShapeDtypeStruct((DEPTH, D_MODEL, Z_W), BF16),
        compiler_params=_params("parallel", "parallel"),
        name="pack_w_in",
    )(w_in)


def _rope_tables():
    half = MLA_ROPE // 2
    pos = jnp.concatenate([jnp.arange(SEQ, dtype=jnp.int32),
                           jnp.tile(PAST_LEN + jnp.arange(DEC_SEQ, dtype=jnp.int32), DEC_BATCH)])
    inv_freq = ROPE_THETA ** (-jnp.arange(half, dtype=F32) / half)
    ang = pos.astype(F32)[:, None] * inv_freq[None, :]
    cos, sin = jnp.cos(ang), jnp.sin(ang)
    c64 = jnp.concatenate([cos, cos], axis=1)
    s64 = jnp.concatenate([-sin, sin], axis=1)
    zero = jnp.zeros_like(s64)
    key = (jnp.concatenate([c64, zero], axis=1), jnp.concatenate([s64, zero], axis=1))
    return (cos.T, sin.T), key


@jax.jit
def _forward(x_prompt, x_sample, cache_ckv, cache_krope, cache_conv, state_gla, p_prompt, p_sample, norm_mix,
             w_in, mla_q_norm, mla_kv_norm, mla_w_uq, mla_w_uk, mla_w_uv, conv_dw_w, conv_dw_b, conv_ln_g,
             conv_ln_b, gla_w_a2, gla_b_a, gla_norm, w_out, norm_ffn, ffn_w_gate, ffn_w_up, ffn_w_down,
             norm_ple, ple_w_gate, ple_w_proj, norm_final):
    w_in_p = pack_w_in(w_in)
    w_uq_t = mla_w_uq.astype(BF16).transpose(0, 2, 1)
    w_uk = mla_w_uk.reshape(DEPTH, MLA_KV_RANK, MLA_HEADS * MLA_NOPE).astype(BF16)
    w_uv_t = mla_w_uv.reshape(DEPTH, MLA_KV_RANK, MLA_HEADS * MLA_V).astype(BF16).transpose(0, 2, 1)
    w_a2 = jnp.concatenate([gla_w_a2.astype(BF16),
                            jnp.zeros((DEPTH, LANES - GLA_GATE_RANK, HK), BF16)], axis=1)
    w_out_b = w_out.astype(BF16)
    w_gate, w_up, w_down = ffn_w_gate.astype(BF16), ffn_w_up.astype(BF16), ffn_w_down.astype(BF16)
    w_pg, w_pp = ple_w_gate.astype(BF16), ple_w_proj.astype(BF16)
    (cos_t, sin_t), (ck, sk) = _rope_tables()

    x = jnp.concatenate([x_prompt.reshape(SEQ, D_MODEL), x_sample.reshape(N_SAMPLE, D_MODEL)], axis=0)
    p_all = jnp.concatenate([p_prompt.reshape(DEPTH, SEQ, D_PLE), p_sample.reshape(DEPTH, N_SAMPLE, D_PLE)], axis=1)
    zero_conv = jnp.zeros((1, CONV_HIST, CONV_CH), F32)
    zero_gla = jnp.zeros((1, GLA_HEADS, GLA_DK, GLA_DV), F32)

    ckvs, krs, convs_p, convs_s, glas_p, glas_s = [], [], [], [], [], []
    xg, ss = norm_prep(x, norm_mix[0])
    for i in range(DEPTH):
        z = in_proj(xg, ss, w_in_p, i)

        qt = mla_q(z, mla_q_norm[i], w_uq_t[i], cos_t, sin_t)
        ckv, kr = mla_latent(z, mla_kv_norm[i], ck, sk)
        k_p, vt_p = kv_proj(ckv, kr, w_uk[i], w_uv_t[i], ATT_T, SEQ)
        pad = SAMPLE_KEYS_PAD - SAMPLE_KEYS
        ckv_s = jnp.concatenate([cache_ckv[i], ckv[SEQ:].reshape(DEC_BATCH, DEC_SEQ, MLA_KV_RANK),
                                 jnp.zeros((DEC_BATCH, pad, MLA_KV_RANK), F32)], axis=1)
        kr_s = jnp.concatenate([cache_krope[i], kr[SEQ:].reshape(DEC_BATCH, DEC_SEQ, MLA_ROPE),
                                jnp.zeros((DEC_BATCH, pad, MLA_ROPE), F32)], axis=1)
        k_s, vt_s = kv_proj(ckv_s.reshape(-1, MLA_KV_RANK), kr_s.reshape(-1, MLA_ROPE), w_uk[i], w_uv_t[i],
                            SAMPLE_KEYS_PAD, DEC_BATCH * SAMPLE_KEYS_PAD)
        k_s = k_s.reshape(MLA_HEADS, DEC_BATCH, SAMPLE_KEYS_PAD, MLA_QK)
        o_a = attn_sample(qt, k_s, vt_s, into=attn_prompt(qt, k_p, vt_p))

        conv_args = (conv_dw_w[i], conv_dw_b[i], conv_ln_g[i], conv_ln_b[i])
        o_b, conv_p = conv_mixer(z, zero_conv, *conv_args, batch=1, seq=SEQ, row_off=0, t_rows=CONV_T)
        o_b, conv_s = conv_mixer(z, cache_conv[i], *conv_args, batch=DEC_BATCH, seq=DEC_SEQ, row_off=SEQ,
                                 t_rows=DEC_SEQ, into=o_b)

        gla_args = (w_a2[i], gla_b_a[i], gla_norm[i])
        o_c, gla_p = gla_mixer(z, zero_gla, *gla_args, batch=1, seq=SEQ, row_off=0, c=CHUNK, cps=GLA_CPS)
        o_c, gla_s = gla_mixer(z, state_gla[i], *gla_args, batch=DEC_BATCH, seq=DEC_SEQ, row_off=SEQ,
                               c=DEC_SEQ, cps=1, into=o_c)

        x, xg, ss = out_proj(o_a, o_b, o_c, w_out_b, x, norm_ffn[i], i)
        x, xg, ss = ffn_down(ffn_gate_up(xg, ss, w_gate, w_up, i), w_down, x, norm_ple[i], i)
        g_next = norm_mix[i + 1] if i + 1 < DEPTH else None
        x, xg, ss = ple_update(xg, ss, p_all, w_pg, w_pp, x, g_next, i)

        ckvs.append(ckv)
        krs.append(kr)
        convs_p.append(conv_p)
        convs_s.append(conv_s)
        glas_p.append(gla_p)
        glas_s.append(gla_s)

    y_prompt = rmsnorm_rows(x, norm_final, 0, SEQ, TY)
    y_sample = rmsnorm_rows(x, norm_final, SEQ, N_SAMPLE, N_SAMPLE)
    ckv_all, kr_all = jnp.stack(ckvs), jnp.stack(krs)
    return (y_prompt.reshape(1, SEQ, D_MODEL),
            y_sample.reshape(DEC_BATCH, DEC_SEQ, D_MODEL),
            ckv_all[:, :SEQ].reshape(DEPTH, 1, SEQ, MLA_KV_RANK),
            kr_all[:, :SEQ].reshape(DEPTH, 1, SEQ, MLA_ROPE),
            jnp.stack(convs_p),
            jnp.stack(glas_p),
            ckv_all[:, SEQ:].reshape(DEPTH, DEC_BATCH, DEC_SEQ, MLA_KV_RANK),
            kr_all[:, SEQ:].reshape(DEPTH, DEC_BATCH, DEC_SEQ, MLA_ROPE),
            jnp.stack(convs_s),
            jnp.stack(glas_s))


def kernel(x_prompt, x_sample, cache_ckv, cache_krope, cache_conv, state_gla, p_prompt, p_sample, norm_mix, w_in, mla_q_norm, mla_kv_norm, mla_w_uq, mla_w_uk, mla_w_uv, conv_dw_w, conv_dw_b, conv_ln_g, conv_ln_b, gla_w_a2, gla_b_a, gla_norm, w_out, norm_ffn, ffn_w_gate, ffn_w_up, ffn_w_down, norm_ple, ple_w_gate, ple_w_proj, norm_final):
    return _forward(x_prompt, x_sample, cache_ckv, cache_krope, cache_conv, state_gla, p_prompt, p_sample,
                    norm_mix, w_in, mla_q_norm, mla_kv_norm, mla_w_uq, mla_w_uk, mla_w_uv, conv_dw_w, conv_dw_b,
                    conv_ln_g, conv_ln_b, gla_w_a2, gla_b_a, gla_norm, w_out, norm_ffn, ffn_w_gate, ffn_w_up,
                    ffn_w_down, norm_ple, ple_w_gate, ple_w_proj, norm_final)
```

```python
import functools

import jax
import jax.numpy as jnp
from jax import lax
from jax.experimental import pallas as pl
from jax.experimental.pallas import tpu as pltpu

F32 = jnp.float32
BF16 = jnp.bfloat16

D_MODEL = 4096
SEQ = 8192
DEPTH = 4
DEC_BATCH = 8
DEC_SEQ = 16
PAST_LEN = 1024
CHUNK = 64
EPS = 1e-6
D_PLE = 256
D_FF = 11008
MLA_HEADS = 16
MLA_Q_RANK = 1024
MLA_KV_RANK = 512
MLA_NOPE = 128
MLA_ROPE = 64
MLA_V = 128
MLA_QK = MLA_NOPE + MLA_ROPE
ROPE_THETA = 10000.0
CONV_CH = 1024
CONV_WIDTH = 31
CONV_HIST = CONV_WIDTH - 1
GLA_HEADS = 4
GLA_DK = 128
GLA_DV = 256
GLA_GATE_RANK = 16
GLA_TAU = 16.0
HK = GLA_HEADS * GLA_DK
HV = GLA_HEADS * GLA_DV

N_SAMPLE = DEC_BATCH * DEC_SEQ
ROWS = SEQ + N_SAMPLE
SAMPLE_KEYS = PAST_LEN + DEC_SEQ
SAMPLE_KEYS_PAD = 1152

LANES = 128
SUBLANES = 8
VMEM_LIMIT = 52 * 1024 * 1024

Z_CA = 0
Z_CB = 1024
Z_GV = 2048
Z_GR = 3072
Z_CQ = 4096
Z_GQ = 5120
Z_GK = 5632
Z_CKV = 6144
Z_KR = 6656
Z_AL = 6784
Z_W = 6912

TM = 832
TR = 416
TY = 512
TN_IN = 768
TN_OUT = 512
TN_FF = 256
TK_DOWN = D_FF // 2
TN_DOWN = 512
TN_PLE = 512
TQ = 640
ATT_T = 512
ATT_HB = 2
Q_SCALE = MLA_QK ** -0.5 * 1.4426950408889634
CONV_T = 128
GLA_CPS = 4
GLA_CP = 64
XPOSE = 128


def _params(*sem):
    return pltpu.CompilerParams(dimension_semantics=sem, vmem_limit_bytes=VMEM_LIMIT)


def _rms(x, g):
    return x * lax.rsqrt(jnp.mean(x * x, axis=-1, keepdims=True) + EPS) * g


def _sigmoid(x):
    return 1.0 / (1.0 + jnp.exp(-x))


def _dot(a, b):
    return jnp.dot(a, b, preferred_element_type=F32)


def _dot_nt(a, b):
    return lax.dot_general(a, b, (((1,), (1,)), ((), ())), preferred_element_type=F32)


def _fold_lanes(sq):
    part = sq[:, :LANES]
    for c in range(LANES, sq.shape[1], LANES):
        part = part + sq[:, c:c + LANES]
    return part


def _emit_norm_inputs(x_new, g_ref, xg_ref, ss_ref, first):
    xg_ref[...] = (x_new * g_ref[...]).astype(BF16)
    part = _fold_lanes(x_new * x_new)

    @pl.when(first)
    def _():
        ss_ref[...] = part

    @pl.when(jnp.logical_not(first))
    def _():
        ss_ref[...] += part


def _row_scale(ss_ref):
    return lax.rsqrt(jnp.sum(ss_ref[...], axis=-1, keepdims=True) * (1.0 / D_MODEL) + EPS)


def _norm_specs(imap_tile, imap_row, tn):
    return (pl.BlockSpec((1, tn), lambda *a: (0, imap_tile(*a)[1])),
            [pl.BlockSpec((TM, tn), imap_tile), pl.BlockSpec((TM, LANES), imap_row)],
            [jax.ShapeDtypeStruct((ROWS, D_MODEL), BF16), jax.ShapeDtypeStruct((ROWS, LANES), F32)])


def _norm_prep_kernel(x_ref, g_ref, xg_ref, ss_ref):
    x = x_ref[...]
    xg_ref[...] = (x * g_ref[...]).astype(BF16)
    ss_ref[...] = _fold_lanes(x * x)


def norm_prep(x, g):
    rows, d = x.shape
    return pl.pallas_call(
        _norm_prep_kernel,
        grid=(rows // TR,),
        in_specs=[pl.BlockSpec((TR, d), lambda i: (i, 0)),
                  pl.BlockSpec((1, d), lambda i: (0, 0))],
        out_specs=[pl.BlockSpec((TR, d), lambda i: (i, 0)), pl.BlockSpec((TR, LANES), lambda i: (i, 0))],
        out_shape=[jax.ShapeDtypeStruct((rows, d), BF16), jax.ShapeDtypeStruct((rows, LANES), F32)],
        compiler_params=_params("parallel"),
        name="norm_prep",
    )(x, g.reshape(1, d))


def _rmsnorm_kernel(x_ref, g_ref, o_ref):
    o_ref[...] = _rms(x_ref[...], g_ref[...]).astype(o_ref.dtype)


def rmsnorm_rows(x, g, row_off, rows, tile):
    d = x.shape[1]
    first = row_off // tile
    return pl.pallas_call(
        _rmsnorm_kernel,
        grid=(rows // tile,),
        in_specs=[pl.BlockSpec((tile, d), lambda i: (first + i, 0)),
                  pl.BlockSpec((1, d), lambda i: (0, 0))],
        out_specs=pl.BlockSpec((tile, d), lambda i: (i, 0)),
        out_shape=jax.ShapeDtypeStruct((rows, d), F32),
        compiler_params=_params("parallel"),
        name="rmsnorm_rows",
    )(x, g.reshape(1, d))


def _in_proj_kernel(xg_ref, ss_ref, w_ref, o_ref):
    o_ref[...] = _dot(xg_ref[...], w_ref[...]) * _row_scale(ss_ref)


def in_proj(xg, ss, w, layer):
    rows, k = xg.shape
    n = w.shape[2]
    return pl.pallas_call(
        _in_proj_kernel,
        grid=(rows // TM, n // TN_IN),
        in_specs=[pl.BlockSpec((TM, k), lambda i, j: (i, 0)),
                  pl.BlockSpec((TM, LANES), lambda i, j: (i, 0)),
                  pl.BlockSpec((None, k, TN_IN), lambda i, j: (layer, 0, j))],
        out_specs=pl.BlockSpec((TM, TN_IN), lambda i, j: (i, j)),
        out_shape=jax.ShapeDtypeStruct((rows, n), F32),
        compiler_params=_params("parallel", "arbitrary"),
        name="in_proj",
    )(xg, ss, w)


def _out_proj_kernel(oa_ref, ob_ref, oc_ref, wa_ref, wb_ref, wc_ref, x_ref, g_ref, o_ref, xg_ref, ss_ref):
    acc = _dot(oa_ref[...], wa_ref[...])
    acc += _dot(ob_ref[...], wb_ref[...])
    acc += _dot(oc_ref[...], wc_ref[...])
    x_new = x_ref[...] + acc
    o_ref[...] = x_new
    _emit_norm_inputs(x_new, g_ref, xg_ref, ss_ref, pl.program_id(1) == 0)


def out_proj(o_a, o_b, o_c, w, x, g_next, layer):
    rows = x.shape[0]
    da, db, dc = o_a.shape[1], o_b.shape[1], o_c.shape[1]
    g_spec, n_specs, n_shapes = _norm_specs(lambda i, j: (i, j), lambda i, j: (i, 0), TN_OUT)
    return pl.pallas_call(
        _out_proj_kernel,
        grid=(rows // TM, D_MODEL // TN_OUT),
        in_specs=[pl.BlockSpec((TM, da), lambda i, j: (i, 0)),
                  pl.BlockSpec((TM, db), lambda i, j: (i, 0)),
                  pl.BlockSpec((TM, dc), lambda i, j: (i, 0)),
                  pl.BlockSpec((None, da, TN_OUT), lambda i, j: (layer, 0, j)),
                  pl.BlockSpec((None, db, TN_OUT), lambda i, j: (layer, da // db, j)),
                  pl.BlockSpec((None, dc, TN_OUT), lambda i, j: (layer, (da + db) // dc, j)),
                  pl.BlockSpec((TM, TN_OUT), lambda i, j: (i, j)),
                  g_spec],
        out_specs=[pl.BlockSpec((TM, TN_OUT), lambda i, j: (i, j))] + n_specs,
        out_shape=[jax.ShapeDtypeStruct((rows, D_MODEL), F32)] + n_shapes,
        compiler_params=_params("parallel", "arbitrary"),
        name="out_proj",
    )(o_a, o_b, o_c, w, w, w, x, g_next.reshape(1, -1))


def _ffn_gate_up_kernel(xg_ref, ss_ref, wg_ref, wu_ref, o_ref):
    xg = xg_ref[...]
    r = _row_scale(ss_ref)
    g = _dot(xg, wg_ref[...].astype(BF16)) * r
    u = _dot(xg, wu_ref[...].astype(BF16)) * r
    o_ref[...] = (g * _sigmoid(g) * u).astype(o_ref.dtype)


def ffn_gate_up(xg, ss, wg, wu, layer):
    rows, k = xg.shape
    return pl.pallas_call(
        _ffn_gate_up_kernel,
        grid=(rows // TM, D_FF // TN_FF),
        in_specs=[pl.BlockSpec((TM, k), lambda i, j: (i, 0)),
                  pl.BlockSpec((TM, LANES), lambda i, j: (i, 0)),
                  pl.BlockSpec((None, k, TN_FF), lambda i, j: (layer, 0, j)),
                  pl.BlockSpec((None, k, TN_FF), lambda i, j: (layer, 0, j))],
        out_specs=pl.BlockSpec((TM, TN_FF), lambda i, j: (i, j)),
        out_shape=jax.ShapeDtypeStruct((rows, D_FF), BF16),
        compiler_params=_params("parallel", "arbitrary"),
        name="ffn_gate_up",
    )(xg, ss, wg, wu)


def _ffn_down_kernel(a_ref, w_ref, x_ref, g_ref, o_ref, xg_ref, ss_ref):
    part = _dot(a_ref[...], w_ref[...])
    k = pl.program_id(2)

    @pl.when(k == 0)
    def _():
        o_ref[...] = x_ref[...] + part

    @pl.when(k != 0)
    def _():
        o_ref[...] += part

    @pl.when(k == pl.num_programs(2) - 1)
    def _():
        _emit_norm_inputs(o_ref[...], g_ref, xg_ref, ss_ref, pl.program_id(1) == 0)


def ffn_down(a, w, x, g_next, layer):
    rows = x.shape[0]
    g_spec, n_specs, n_shapes = _norm_specs(lambda i, j, k: (i, j), lambda i, j, k: (i, 0), TN_DOWN)
    return pl.pallas_call(
        _ffn_down_kernel,
        grid=(rows // TM, D_MODEL // TN_DOWN, D_FF // TK_DOWN),
        in_specs=[pl.BlockSpec((TM, TK_DOWN), lambda i, j, k: (i, k)),
                  pl.BlockSpec((None, TK_DOWN, TN_DOWN), lambda i, j, k: (layer, k, j)),
                  pl.BlockSpec((TM, TN_DOWN), lambda i, j, k: (i, j)),
                  g_spec],
        out_specs=[pl.BlockSpec((TM, TN_DOWN), lambda i, j, k: (i, j))] + n_specs,
        out_shape=[jax.ShapeDtypeStruct((rows, D_MODEL), F32)] + n_shapes,
        compiler_params=_params("parallel", "arbitrary", "arbitrary"),
        name="ffn_down",
    )(a, w, x, g_next.reshape(1, -1))


def _ple_kernel(xg_ref, ss_ref, p_ref, wg_ref, wp_ref, x_ref, *rest, emit):
    gate = _sigmoid(_dot(xg_ref[...], wg_ref[...]) * _row_scale(ss_ref))
    proj = _dot(p_ref[...].astype(BF16), wp_ref[...])
    x_new = x_ref[...] + gate * proj
    if emit:
        g_ref, o_ref, xg_out_ref, ss_out_ref = rest
        _emit_norm_inputs(x_new, g_ref, xg_out_ref, ss_out_ref, pl.program_id(1) == 0)
    else:
        (o_ref,) = rest
    o_ref[...] = x_new


def ple_update(xg, ss, p, wg, wp, x, g_next, layer):
    rows, k = xg.shape
    emit = g_next is not None
    g_spec, n_specs, n_shapes = _norm_specs(lambda i, j: (i, j), lambda i, j: (i, 0), TN_PLE)
    outs = pl.pallas_call(
        functools.partial(_ple_kernel, emit=emit),
        grid=(rows // TM, D_MODEL // TN_PLE),
        in_specs=[pl.BlockSpec((TM, k), lambda i, j: (i, 0)),
                  pl.BlockSpec((TM, LANES), lambda i, j: (i, 0)),
                  pl.BlockSpec((None, TM, D_PLE), lambda i, j: (layer, i, 0)),
                  pl.BlockSpec((None, k, TN_PLE), lambda i, j: (layer, 0, j)),
                  pl.BlockSpec((None, D_PLE, TN_PLE), lambda i, j: (layer, 0, j)),
                  pl.BlockSpec((TM, TN_PLE), lambda i, j: (i, j))] + ([g_spec] if emit else []),
        out_specs=[pl.BlockSpec((TM, TN_PLE), lambda i, j: (i, j))] + (n_specs if emit else []),
        out_shape=[jax.ShapeDtypeStruct((rows, D_MODEL), F32)] + (n_shapes if emit else []),
        compiler_params=_params("parallel", "arbitrary"),
        name="ple_update",
    )(xg, ss, p, wg, wp, x, *([g_next.reshape(1, -1)] if emit else []))
    return outs if emit else (outs[0], None, None)


def _swap_rope_halves(x):
    lane = lax.broadcasted_iota(jnp.int32, x.shape, 1)
    first = (lane % MLA_ROPE) < (MLA_ROPE // 2)
    return jnp.where(first, pltpu.roll(x, LANES - MLA_ROPE // 2, 1), pltpu.roll(x, MLA_ROPE // 2, 1))


def _mla_q_kernel(z_ref, g_ref, wt_ref, cos_ref, sin_ref, qt_ref):
    cq = _rms(z_ref[...], g_ref[...]).astype(BF16)
    qt = _dot_nt(wt_ref[...], cq)
    cos, sin = cos_ref[...], sin_ref[...]
    half = MLA_ROPE // 2
    for h in range(MLA_HEADS):
        base = h * MLA_QK
        x1 = qt[base + MLA_NOPE:base + MLA_NOPE + half]
        x2 = qt[base + MLA_NOPE + half:base + MLA_QK]
        qt_ref[h, :MLA_NOPE, :] = (qt[base:base + MLA_NOPE] * Q_SCALE).astype(BF16)
        qt_ref[h, MLA_NOPE:MLA_NOPE + half, :] = ((x1 * cos - x2 * sin) * Q_SCALE).astype(BF16)
        qt_ref[h, MLA_NOPE + half:, :] = ((x2 * cos + x1 * sin) * Q_SCALE).astype(BF16)


def mla_q(z, g, w_uq_t, cos_t, sin_t):
    rows = z.shape[0]
    tab_spec = pl.BlockSpec((MLA_ROPE // 2, TQ), lambda i: (0, i))
    return pl.pallas_call(
        _mla_q_kernel,
        grid=(rows // TQ,),
        in_specs=[pl.BlockSpec((TQ, MLA_Q_RANK), lambda i: (i, Z_CQ // MLA_Q_RANK)),
                  pl.BlockSpec((1, MLA_Q_RANK), lambda i: (0, 0)),
                  pl.BlockSpec(w_uq_t.shape, lambda i: (0, 0)),
                  tab_spec, tab_spec],
        out_specs=pl.BlockSpec((MLA_HEADS, MLA_QK, TQ), lambda i: (0, 0, i)),
        out_shape=jax.ShapeDtypeStruct((MLA_HEADS, MLA_QK, rows), BF16),
        compiler_params=_params("parallel"),
        name="mla_q",
    )(z, g.reshape(1, -1), w_uq_t, cos_t, sin_t)


def _mla_latent_kernel(zc_ref, zk_ref, g_ref, ck_ref, sk_ref, ckv_ref, kr_ref):
    ckv_ref[...] = _rms(zc_ref[...], g_ref[...])
    kr = zk_ref[...]
    rot = kr * ck_ref[...] + _swap_rope_halves(kr) * sk_ref[...]
    kr_ref[...] = rot[:, :MLA_ROPE]


def mla_latent(z, g, ck, sk):
    rows = z.shape[0]
    return pl.pallas_call(
        _mla_latent_kernel,
        grid=(rows // TR,),
        in_specs=[pl.BlockSpec((TR, MLA_KV_RANK), lambda i: (i, Z_CKV // MLA_KV_RANK)),
                  pl.BlockSpec((TR, LANES), lambda i: (i, Z_KR // LANES)),
                  pl.BlockSpec((1, MLA_KV_RANK), lambda i: (0, 0)),
                  pl.BlockSpec((TR, LANES), lambda i: (i, 0)),
                  pl.BlockSpec((TR, LANES), lambda i: (i, 0))],
        out_specs=[pl.BlockSpec((TR, MLA_KV_RANK), lambda i: (i, 0)),
                   pl.BlockSpec((TR, MLA_ROPE), lambda i: (i, 0))],
        out_shape=[jax.ShapeDtypeStruct((rows, MLA_KV_RANK), F32),
                   jax.ShapeDtypeStruct((rows, MLA_ROPE), F32)],
        compiler_params=_params("parallel"),
        name="mla_latent",
    )(z, z, g.reshape(1, -1), ck, sk)


def _kv_proj_kernel(ckv_ref, kr_ref, wk_ref, wvt_ref, k_ref, vt_ref):
    c = ckv_ref[...].astype(BF16)
    kn = _dot(c, wk_ref[...])
    vt = _dot_nt(wvt_ref[...], c)
    kr = kr_ref[...].astype(BF16)
    for h in range(MLA_HEADS):
        k_ref[h, :, :MLA_NOPE] = kn[:, h * MLA_NOPE:(h + 1) * MLA_NOPE].astype(BF16)
        k_ref[h, :, MLA_NOPE:] = kr
        vt_ref[h, 0] = vt[h * MLA_V:(h + 1) * MLA_V].astype(BF16)


def kv_proj(ckv, kr, w_uk, w_uv_t, tile, rows):
    return pl.pallas_call(
        _kv_proj_kernel,
        grid=(rows // tile,),
        in_specs=[pl.BlockSpec((tile, MLA_KV_RANK), lambda i: (i, 0)),
                  pl.BlockSpec((tile, MLA_ROPE), lambda i: (i, 0)),
                  pl.BlockSpec(w_uk.shape, lambda i: (0, 0)),
                  pl.BlockSpec(w_uv_t.shape, lambda i: (0, 0))],
        out_specs=[pl.BlockSpec((MLA_HEADS, tile, MLA_QK), lambda i: (0, i, 0)),
                   pl.BlockSpec((MLA_HEADS, 1, MLA_V, tile), lambda i: (0, i, 0, 0))],
        out_shape=[jax.ShapeDtypeStruct((MLA_HEADS, rows, MLA_QK), BF16),
                   jax.ShapeDtypeStruct((MLA_HEADS, rows // tile, MLA_V, tile), BF16)],
        compiler_params=_params("parallel"),
        name="kv_proj",
    )(ckv, kr, w_uk, w_uv_t)


def _attn_prompt_kernel(qt_ref, k_ref, vt_ref, o_ref, sa_sc, sb_sc, m_sc, l_sc, acc_sc):
    i = pl.program_id(1)
    m_sc[...] = jnp.full(m_sc.shape, -jnp.inf, F32)
    l_sc[...] = jnp.zeros(l_sc.shape, F32)
    acc_sc[...] = jnp.zeros(acc_sc.shape, F32)

    def scores(j, s_sc):
        start = pl.multiple_of(j * ATT_T, ATT_T)
        for hb in range(ATT_HB):
            s_sc[hb] = _dot(k_ref[hb, pl.ds(start, ATT_T), :], qt_ref[hb])

    def step(j, s_sc, masked):
        for hb in range(ATT_HB):
            s = s_sc[hb]
            if masked:
                kc = lax.broadcasted_iota(jnp.int32, s.shape, 0) // CHUNK
                qc = lax.broadcasted_iota(jnp.int32, s.shape, 1) // CHUNK
                s = jnp.where(kc <= qc, s, -jnp.inf)
            m_prev = m_sc[hb]
            m_new = jnp.maximum(m_prev, jnp.max(s, axis=0, keepdims=True))
            alpha = jnp.exp2(m_prev - m_new)
            p = jnp.exp2(s - m_new)
            l_sc[hb] = alpha * l_sc[hb] + jnp.sum(p, axis=0, keepdims=True)
            acc_sc[hb] = alpha * acc_sc[hb] + _dot(vt_ref[hb, j], p.astype(BF16))
            m_sc[hb] = m_new

    def pair(jj, carry):
        j = 2 * jj
        scores(j + 1, sb_sc)
        step(j, sa_sc, False)
        scores(j + 2, sa_sc)
        step(j + 1, sb_sc, False)
        return carry

    scores(0, sa_sc)
    lax.fori_loop(0, i // 2, pair, 0)

    @pl.when(i % 2 == 0)
    def _():
        step(i, sa_sc, True)

    @pl.when(i % 2 == 1)
    def _():
        scores(i, sb_sc)
        step(i - 1, sa_sc, False)
        step(i, sb_sc, True)

    for hb in range(ATT_HB):
        out_t = acc_sc[hb] / l_sc[hb]
        o_ref[:, hb * MLA_V:(hb + 1) * MLA_V] = out_t.T.astype(o_ref.dtype)


def attn_prompt(qt, k, vt):
    nblk = SEQ // ATT_T
    return pl.pallas_call(
        _attn_prompt_kernel,
        grid=(MLA_HEADS // ATT_HB, nblk),
        in_specs=[pl.BlockSpec((ATT_HB, MLA_QK, ATT_T), lambda g, i: (g, 0, i)),
                  pl.BlockSpec((ATT_HB, SEQ, MLA_QK), lambda g, i: (g, 0, 0)),
                  pl.BlockSpec((ATT_HB, nblk, MLA_V, ATT_T), lambda g, i: (g, 0, 0, 0))],
        out_specs=pl.BlockSpec((ATT_T, ATT_HB * MLA_V), lambda g, i: (i, g)),
        out_shape=jax.ShapeDtypeStruct((ROWS, MLA_HEADS * MLA_V), BF16),
        scratch_shapes=[pltpu.VMEM((ATT_HB, ATT_T, ATT_T), F32), pltpu.VMEM((ATT_HB, ATT_T, ATT_T), F32),
                        pltpu.VMEM((ATT_HB, 1, ATT_T), F32), pltpu.VMEM((ATT_HB, 1, ATT_T), F32),
                        pltpu.VMEM((ATT_HB, MLA_V, ATT_T), F32)],
        compiler_params=_params("parallel", "arbitrary"),
        name="attn_prompt",
    )(qt, k, vt)


def _attn_sample_kernel(qt_ref, k_ref, vt_ref, into_ref, o_ref):
    del into_ref
    qt = qt_ref[0]
    shape = (SAMPLE_KEYS_PAD, N_SAMPLE)
    kpos = lax.broadcasted_iota(jnp.int32, shape, 0)
    qpos = PAST_LEN + lax.broadcasted_iota(jnp.int32, shape, 1) % DEC_SEQ
    kchunk = jnp.where(kpos < SAMPLE_KEYS, kpos // CHUNK, SAMPLE_KEYS_PAD)
    visible = kchunk <= qpos // CHUNK
    stream = lax.broadcasted_iota(jnp.int32, (MLA_V, N_SAMPLE), 1) // DEC_SEQ
    out_t = jnp.zeros((MLA_V, N_SAMPLE), F32)
    for b in range(DEC_BATCH):
        s = jnp.where(visible, _dot(k_ref[0, b], qt), -jnp.inf)
        p = jnp.exp2(s - jnp.max(s, axis=0, keepdims=True))
        o_b = _dot(vt_ref[0, b], p.astype(BF16)) / jnp.sum(p, axis=0, keepdims=True)
        out_t = jnp.where(stream == b, o_b, out_t)
    o_ref[...] = out_t.T.astype(o_ref.dtype)


def attn_sample(qt, k, vt, into):
    return pl.pallas_call(
        _attn_sample_kernel,
        grid=(MLA_HEADS,),
        in_specs=[pl.BlockSpec((1, MLA_QK, N_SAMPLE), lambda h: (h, 0, SEQ // N_SAMPLE)),
                  pl.BlockSpec((1, DEC_BATCH, SAMPLE_KEYS_PAD, MLA_QK), lambda h: (h, 0, 0, 0)),
                  pl.BlockSpec((1, DEC_BATCH, MLA_V, SAMPLE_KEYS_PAD), lambda h: (h, 0, 0, 0)),
                  pl.BlockSpec(memory_space=pl.ANY)],
        out_specs=pl.BlockSpec((N_SAMPLE, MLA_V), lambda h: (SEQ // N_SAMPLE, h)),
        out_shape=jax.ShapeDtypeStruct(into.shape, into.dtype),
        input_output_aliases={3: 0},
        compiler_params=_params("parallel"),
        name="attn_sample",
    )(qt, k, vt, into)


CONV_PAD = 32
CONV_OFF = CONV_PAD - CONV_HIST


def _conv_kernel(za_ref, zb_ref, past_ref, w_ref, b_ref, g_ref, beta_ref, *rest, t_rows):
    o_ref, hist_ref, ext_sc = rest[-3:]

    @pl.when(pl.program_id(1) == 0)
    def _():
        ext_sc[0:SUBLANES, :] = jnp.zeros((SUBLANES, CONV_CH), F32)
        ext_sc[CONV_OFF:CONV_PAD, :] = past_ref[0]

    u = za_ref[...] * _sigmoid(zb_ref[...])
    ext_sc[CONV_PAD:CONV_PAD + t_rows, :] = u
    cols = []
    for c0 in range(0, CONV_CH, LANES):
        acc = None
        for shift in range(SUBLANES):
            win = t_rows + (SUBLANES if shift else 0)
            group = None
            for base in range(0, CONV_PAD + SUBLANES, SUBLANES):
                tap = base + shift - CONV_OFF
                if 0 <= tap < CONV_WIDTH:
                    term = w_ref[tap:tap + 1, c0:c0 + LANES] * ext_sc[base:base + win, c0:c0 + LANES]
                    group = term if group is None else group + term
            part = group[shift:shift + t_rows]
            acc = part if acc is None else acc + part
        cols.append(acc)
    y = jnp.concatenate(cols, axis=1) + b_ref[...]
    yc = y - jnp.mean(y, axis=-1, keepdims=True)
    y = yc * lax.rsqrt(jnp.mean(yc * yc, axis=-1, keepdims=True) + EPS) * g_ref[...] + beta_ref[...]
    o_ref[...] = (y * _sigmoid(y)).astype(o_ref.dtype)
    tail = ext_sc[t_rows + CONV_OFF:t_rows + CONV_PAD, :]
    hist_ref[0] = tail
    ext_sc[CONV_OFF:CONV_PAD, :] = tail


def _into_args(into):
    if into is None:
        return [], [], None
    return [pl.BlockSpec(memory_space=pl.ANY)], [into], into


def conv_mixer(z, past, w, b, g, beta, *, batch, seq, row_off, t_rows, into=None):
    steps = seq // t_rows
    first = row_off // t_rows

    def rows(bi, t):
        return first + bi * steps + t

    vec = pl.BlockSpec((1, CONV_CH), lambda bi, t: (0, 0))
    extra_specs, extra_args, alias = _into_args(into)
    return pl.pallas_call(
        functools.partial(_conv_kernel, t_rows=t_rows),
        grid=(batch, steps),
        in_specs=[pl.BlockSpec((t_rows, CONV_CH), lambda bi, t: (rows(bi, t), Z_CA // CONV_CH)),
                  pl.BlockSpec((t_rows, CONV_CH), lambda bi, t: (rows(bi, t), Z_CB // CONV_CH)),
                  pl.BlockSpec((1, CONV_HIST, CONV_CH), lambda bi, t: (bi, 0, 0)),
                  pl.BlockSpec((CONV_WIDTH, CONV_CH), lambda bi, t: (0, 0)),
                  vec, vec, vec] + extra_specs,
        out_specs=[pl.BlockSpec((t_rows, CONV_CH), lambda bi, t: (rows(bi, t), 0)),
                   pl.BlockSpec((1, CONV_HIST, CONV_CH), lambda bi, t: (bi, 0, 0))],
        out_shape=[jax.ShapeDtypeStruct((ROWS, CONV_CH), BF16),
                   jax.ShapeDtypeStruct((batch, CONV_HIST, CONV_CH), F32)],
        input_output_aliases={} if alias is None else {7: 0},
        scratch_shapes=[pltpu.VMEM((CONV_PAD + t_rows, CONV_CH), F32)],
        compiler_params=_params("parallel", "arbitrary"),
        name="conv_mixer",
    )(z, z, past, w, b.reshape(1, -1), g.reshape(1, -1), beta.reshape(1, -1), *extra_args)


def _pad_rows(x, rows):
    if x.shape[0] == rows:
        return x
    return jnp.concatenate([x, jnp.zeros((rows - x.shape[0], x.shape[1]), x.dtype)], axis=0)


def _log_sigmoid(x):
    return jnp.minimum(x, 0.0) - jnp.log1p(jnp.exp(-jnp.abs(x)))


def _gla_kernel(q_ref, k_ref, v_ref, r_ref, al_ref, wa_ref, ba_ref, gn_ref, s0_ref, *rest, c, cps):
    o_ref, s_out_ref, s_sc = rest[-3:]
    step = pl.program_id(1)

    @pl.when(step == 0)
    def _():
        s_sc[...] = s0_ref[0]

    row = lax.broadcasted_iota(jnp.int32, (GLA_CP, GLA_CP), 0)
    col = lax.broadcasted_iota(jnp.int32, (GLA_CP, GLA_CP), 1)
    causal = col <= row
    tri = causal.astype(F32)
    for ci in range(cps):
        r0 = ci * c
        gate = _dot(al_ref[r0:r0 + c, :].astype(BF16), wa_ref[...]) + ba_ref[...]
        log_a = _pad_rows(_log_sigmoid(gate) / GLA_TAU, GLA_CP)
        bcum = jnp.dot(tri, log_a, precision=lax.Precision.HIGHEST, preferred_element_type=F32)
        for h in range(GLA_HEADS):
            ks = slice(h * GLA_DK, (h + 1) * GLA_DK)
            vs = slice(h * GLA_DV, (h + 1) * GLA_DV)
            b = bcum[:, ks]
            q = _pad_rows(q_ref[r0:r0 + c, ks], GLA_CP) * GLA_DK ** -0.5
            k = _pad_rows(k_ref[r0:r0 + c, ks], GLA_CP)
            v = _pad_rows(v_ref[r0:r0 + c, vs], GLA_CP).astype(BF16)
            q_t = (q * jnp.exp(b)).astype(BF16)
            k_t = (k * jnp.exp(-b)).astype(BF16)
            a = jnp.where(causal, _dot_nt(q_t, k_t), 0.0)
            s_prev = s_sc[h]
            o = _dot(q_t, s_prev.astype(BF16)) + _dot(a.astype(BF16), v)
            b_last = b[c - 1:c, :]
            k_dec = k * jnp.exp(b_last - b)
            k_dec_t = _pad_rows(k_dec, XPOSE).T
            decay = jnp.exp(_pad_rows(b, XPOSE).T[:, c - 1:c])
            s_sc[h] = decay * s_prev + _dot(k_dec_t.astype(BF16), _pad_rows(v, XPOSE))
            on = _rms(o[:c], gn_ref[...])
            r = r_ref[r0:r0 + c, vs]
            o_ref[r0:r0 + c, vs] = (on * (r * _sigmoid(r))).astype(o_ref.dtype)

    @pl.when(step == pl.num_programs(1) - 1)
    def _():
        s_out_ref[0] = s_sc[...]


def gla_mixer(z, s0, wa2, ba, gn, *, batch, seq, row_off, c, cps, into=None):
    t_rows = c * cps
    steps = seq // t_rows
    first = row_off // t_rows

    def rows(bi, t):
        return first + bi * steps + t

    state = pl.BlockSpec((1, GLA_HEADS, GLA_DK, GLA_DV), lambda bi, t: (bi, 0, 0, 0))
    extra_specs, extra_args, alias = _into_args(into)
    return pl.pallas_call(
        functools.partial(_gla_kernel, c=c, cps=cps),
        grid=(batch, steps),
        in_specs=[pl.BlockSpec((t_rows, HK), lambda bi, t: (rows(bi, t), Z_GQ // HK)),
                  pl.BlockSpec((t_rows, HK), lambda bi, t: (rows(bi, t), Z_GK // HK)),
                  pl.BlockSpec((t_rows, HV), lambda bi, t: (rows(bi, t), Z_GV // HV)),
                  pl.BlockSpec((t_rows, HV), lambda bi, t: (rows(bi, t), Z_GR // HV)),
                  pl.BlockSpec((t_rows, LANES), lambda bi, t: (rows(bi, t), Z_AL // LANES)),
                  pl.BlockSpec((LANES, HK), lambda bi, t: (0, 0)),
                  pl.BlockSpec((1, HK), lambda bi, t: (0, 0)),
                  pl.BlockSpec((1, GLA_DV), lambda bi, t: (0, 0)),
                  state] + extra_specs,
        out_specs=[pl.BlockSpec((t_rows, HV), lambda bi, t: (rows(bi, t), 0)), state],
        out_shape=[jax.ShapeDtypeStruct((ROWS, HV), BF16),
                   jax.ShapeDtypeStruct((batch, GLA_HEADS, GLA_DK, GLA_DV), F32)],
        input_output_aliases={} if alias is None else {9: 0},
        scratch_shapes=[pltpu.VMEM((GLA_HEADS, GLA_DK, GLA_DV), F32)],
        compiler_params=_params("parallel", "arbitrary"),
        name="gla_mixer",
    )(z, z, z, z, z, wa2, ba.reshape(1, -1), gn.reshape(1, -1), s0, *extra_args)


_SRC_CONV = MLA_Q_RANK + MLA_KV_RANK + MLA_ROPE
_SRC_GLA = _SRC_CONV + 2 * CONV_CH
D_IN = _SRC_GLA + 2 * HK + 2 * HV + GLA_GATE_RANK
_W_IN_PIECES = (
    (_SRC_CONV, CONV_CH, Z_CA), (_SRC_CONV + CONV_CH, CONV_CH, Z_CB),
    (_SRC_GLA + 2 * HK, HV, Z_GV), (_SRC_GLA + 2 * HK + HV, HV, Z_GR),
    (0, MLA_Q_RANK, Z_CQ),
    (_SRC_GLA, HK, Z_GQ), (_SRC_GLA + HK, HK, Z_GK),
    (MLA_Q_RANK, MLA_KV_RANK, Z_CKV),
    (MLA_Q_RANK + MLA_KV_RANK, MLA_ROPE, Z_KR),
    (_SRC_GLA + 2 * HK + 2 * HV, GLA_GATE_RANK, Z_AL),
)
PACK_SRC = 64


def _pack_tables():
    first, valid = [], []
    for col in range(0, Z_W, LANES):
        src, width, dst = next(p for p in _W_IN_PIECES if p[2] <= col < p[2] + max(p[1], LANES))
        first.append((src + col - dst) // PACK_SRC)
        valid.append(min(LANES, width))
    return jnp.asarray(first, jnp.int32), jnp.asarray(valid, jnp.int32)


def _pack_w_in_kernel(first_ref, valid_ref, lo_ref, hi_ref, o_ref):
    del first_ref
    x = jnp.concatenate([lo_ref[...], hi_ref[...]], axis=0)
    row = lax.broadcasted_iota(jnp.int32, x.shape, 0)
    x = jnp.where(row < valid_ref[pl.program_id(1)], x, 0.0)
    o_ref[...] = x.T.astype(BF16)


def pack_w_in(w_in_t):
    first, valid = _pack_tables()
    last = (D_IN - 1) // PACK_SRC
    return pl.pallas_call(
        _pack_w_in_kernel,
        grid_spec=pltpu.PrefetchScalarGridSpec(
            num_scalar_prefetch=2,
            grid=(DEPTH, Z_W // LANES),
            in_specs=[pl.BlockSpec((None, PACK_SRC, D_MODEL), lambda l, c, first, valid: (l, first[c], 0)),
                      pl.BlockSpec((None, PACK_SRC, D_MODEL),
                                   lambda l, c, first, valid: (l, jnp.minimum(first[c] + 1, last), 0))],
            out_specs=pl.BlockSpec((None, D_MODEL, LANES), lambda l, c, first, valid: (l, 0, c)),
        ),
        out_shape=jax.ShapeDtypeStruct((DEPTH, D_MODEL, Z_W), BF16),
        compiler_params=_params("parallel", "arbitrary"),
        name="pack_w_in",
    )(first, valid, w_in_t, w_in_t)


def _rope_tables():
    half = MLA_ROPE // 2
    pos = jnp.concatenate([jnp.arange(SEQ, dtype=jnp.int32),
                           jnp.tile(PAST_LEN + jnp.arange(DEC_SEQ, dtype=jnp.int32), DEC_BATCH)])
    inv_freq = ROPE_THETA ** (-jnp.arange(half, dtype=F32) / half)
    ang = pos.astype(F32)[:, None] * inv_freq[None, :]
    cos, sin = jnp.cos(ang), jnp.sin(ang)
    c64 = jnp.concatenate([cos, cos], axis=1)
    s64 = jnp.concatenate([-sin, sin], axis=1)
    zero = jnp.zeros_like(s64)
    key = (jnp.concatenate([c64, zero], axis=1), jnp.concatenate([s64, zero], axis=1))
    return (cos.T, sin.T), key


@jax.jit
def _forward(x_prompt, x_sample, cache_ckv, cache_krope, cache_conv, state_gla, p_prompt, p_sample, norm_mix,
             w_in, mla_q_norm, mla_kv_norm, mla_w_uq, mla_w_uk, mla_w_uv, conv_dw_w, conv_dw_b, conv_ln_g,
             conv_ln_b, gla_w_a2, gla_b_a, gla_norm, w_out, norm_ffn, ffn_w_gate, ffn_w_up, ffn_w_down,
             norm_ple, ple_w_gate, ple_w_proj, norm_final):
    w_in_p = pack_w_in(jnp.swapaxes(w_in, 1, 2))
    w_uq_t = mla_w_uq.astype(BF16).transpose(0, 2, 1)
    w_uk = mla_w_uk.reshape(DEPTH, MLA_KV_RANK, MLA_HEADS * MLA_NOPE).astype(BF16)
    w_uv_t = mla_w_uv.reshape(DEPTH, MLA_KV_RANK, MLA_HEADS * MLA_V).astype(BF16).transpose(0, 2, 1)
    w_a2 = jnp.concatenate([gla_w_a2.astype(BF16),
                            jnp.zeros((DEPTH, LANES - GLA_GATE_RANK, HK), BF16)], axis=1)
    w_out_b = w_out.astype(BF16)
    w_gate, w_up = ffn_w_gate, ffn_w_up
    w_down = ffn_w_down.astype(BF16)
    w_pg, w_pp = ple_w_gate.astype(BF16), ple_w_proj.astype(BF16)
    (cos_t, sin_t), (ck, sk) = _rope_tables()

    x = jnp.concatenate([x_prompt.reshape(SEQ, D_MODEL), x_sample.reshape(N_SAMPLE, D_MODEL)], axis=0)
    p_all = jnp.concatenate([p_prompt.reshape(DEPTH, SEQ, D_PLE), p_sample.reshape(DEPTH, N_SAMPLE, D_PLE)], axis=1)
    zero_conv = jnp.zeros((1, CONV_HIST, CONV_CH), F32)
    zero_gla = jnp.zeros((1, GLA_HEADS, GLA_DK, GLA_DV), F32)

    ckvs, krs, convs_p, convs_s, glas_p, glas_s = [], [], [], [], [], []
    xg, ss = norm_prep(x, norm_mix[0])
    for i in range(DEPTH):
        z = in_proj(xg, ss, w_in_p, i)

        qt = mla_q(z, mla_q_norm[i], w_uq_t[i], cos_t, sin_t)
        ckv, kr = mla_latent(z, mla_kv_norm[i], ck, sk)
        k_p, vt_p = kv_proj(ckv, kr, w_uk[i], w_uv_t[i], ATT_T, SEQ)
        pad = SAMPLE_KEYS_PAD - SAMPLE_KEYS
        ckv_s = jnp.concatenate([cache_ckv[i], ckv[SEQ:].reshape(DEC_BATCH, DEC_SEQ, MLA_KV_RANK),
                                 jnp.zeros((DEC_BATCH, pad, MLA_KV_RANK), F32)], axis=1)
        kr_s = jnp.concatenate([cache_krope[i], kr[SEQ:].reshape(DEC_BATCH, DEC_SEQ, MLA_ROPE),
                                jnp.zeros((DEC_BATCH, pad, MLA_ROPE), F32)], axis=1)
        k_s, vt_s = kv_proj(ckv_s.reshape(-1, MLA_KV_RANK), kr_s.reshape(-1, MLA_ROPE), w_uk[i], w_uv_t[i],
                            SAMPLE_KEYS_PAD, DEC_BATCH * SAMPLE_KEYS_PAD)
        k_s = k_s.reshape(MLA_HEADS, DEC_BATCH, SAMPLE_KEYS_PAD, MLA_QK)
        o_a = attn_sample(qt, k_s, vt_s, into=attn_prompt(qt, k_p, vt_p))

        conv_args = (conv_dw_w[i], conv_dw_b[i], conv_ln_g[i], conv_ln_b[i])
        o_b, conv_p = conv_mixer(z, zero_conv, *conv_args, batch=1, seq=SEQ, row_off=0, t_rows=CONV_T)
        o_b, conv_s = conv_mixer(z, cache_conv[i], *conv_args, batch=DEC_BATCH, seq=DEC_SEQ, row_off=SEQ,
                                 t_rows=DEC_SEQ, into=o_b)

        gla_args = (w_a2[i], gla_b_a[i], gla_norm[i])
        o_c, gla_p = gla_mixer(z, zero_gla, *gla_args, batch=1, seq=SEQ, row_off=0, c=CHUNK, cps=GLA_CPS)
        o_c, gla_s = gla_mixer(z, state_gla[i], *gla_args, batch=DEC_BATCH, seq=DEC_SEQ, row_off=SEQ,
                               c=DEC_SEQ, cps=1, into=o_c)

        x, xg, ss = out_proj(o_a, o_b, o_c, w_out_b, x, norm_ffn[i], i)
        x, xg, ss = ffn_down(ffn_gate_up(xg, ss, w_gate, w_up, i), w_down, x, norm_ple[i], i)
        g_next = norm_mix[i + 1] if i + 1 < DEPTH else None
        x, xg, ss = ple_update(xg, ss, p_all, w_pg, w_pp, x, g_next, i)

        ckvs.append(ckv)
        krs.append(kr)
        convs_p.append(conv_p)
        convs_s.append(conv_s)
        glas_p.append(gla_p)
        glas_s.append(gla_s)

    y_prompt = rmsnorm_rows(x, norm_final, 0, SEQ, TY)
    y_sample = rmsnorm_rows(x, norm_final, SEQ, N_SAMPLE, N_SAMPLE)
    ckv_all, kr_all = jnp.stack(ckvs), jnp.stack(krs)
    return (y_prompt.reshape(1, SEQ, D_MODEL),
            y_sample.reshape(DEC_BATCH, DEC_SEQ, D_MODEL),
            ckv_all[:, :SEQ].reshape(DEPTH, 1, SEQ, MLA_KV_RANK),
            kr_all[:, :SEQ].reshape(DEPTH, 1, SEQ, MLA_ROPE),
            jnp.stack(convs_p),
            jnp.stack(glas_p),
            ckv_all[:, SEQ:].reshape(DEPTH, DEC_BATCH, DEC_SEQ, MLA_KV_RANK),
            kr_all[:, SEQ:].reshape(DEPTH, DEC_BATCH, DEC_SEQ, MLA_ROPE),
            jnp.stack(convs_s),
            jnp.stack(glas_s))


def kernel(x_prompt, x_sample, cache_ckv, cache_krope, cache_conv, state_gla, p_prompt, p_sample, norm_mix, w_in, mla_q_norm, mla_kv_norm, mla_w_uq, mla_w_uk, mla_w_uv, conv_dw_w, conv_dw_b, conv_ln_g, conv_ln_b, gla_w_a2, gla_b_a, gla_norm, w_out, norm_ffn, ffn_w_gate, ffn_w_up, ffn_w_down, norm_ple, ple_w_gate, ple_w_proj, norm_final):
    return _forward(x_prompt, x_sample, cache_ckv, cache_krope, cache_conv, state_gla, p_prompt, p_sample,
                    norm_mix, w_in, mla_q_norm, mla_kv_norm, mla_w_uq, mla_w_uk, mla_w_uv, conv_dw_w, conv_dw_b,
                    conv_ln_g, conv_ln_b, gla_w_a2, gla_b_a, gla_norm, w_out, norm_ffn, ffn_w_gate, ffn_w_up,
                    ffn_w_down, norm_ple, ple_w_gate, ple_w_proj, norm_final)
```

```python
import functools

import jax
import jax.numpy as jnp
from jax import lax
from jax.experimental import pallas as pl
from jax.experimental.pallas import tpu as pltpu

F32 = jnp.float32
BF16 = jnp.bfloat16

D_MODEL = 4096
SEQ = 8192
DEPTH = 4
DEC_BATCH = 8
DEC_SEQ = 16
PAST_LEN = 1024
CHUNK = 64
EPS = 1e-6
D_PLE = 256
D_FF = 11008
MLA_HEADS = 16
MLA_Q_RANK = 1024
MLA_KV_RANK = 512
MLA_NOPE = 128
MLA_ROPE = 64
MLA_V = 128
MLA_QK = MLA_NOPE + MLA_ROPE
ROPE_THETA = 10000.0
CONV_CH = 1024
CONV_WIDTH = 31
CONV_HIST = CONV_WIDTH - 1
GLA_HEADS = 4
GLA_DK = 128
GLA_DV = 256
GLA_GATE_RANK = 16
GLA_TAU = 16.0
HK = GLA_HEADS * GLA_DK
HV = GLA_HEADS * GLA_DV

N_SAMPLE = DEC_BATCH * DEC_SEQ
ROWS = SEQ + N_SAMPLE
SAMPLE_KEYS = PAST_LEN + DEC_SEQ
SAMPLE_KEYS_PAD = 1152

LANES = 128
SUBLANES = 8
VMEM_LIMIT = 52 * 1024 * 1024

Z_CA = 0
Z_CB = 1024
Z_GV = 2048
Z_GR = 3072
Z_CQ = 4096
Z_GQ = 5120
Z_GK = 5632
Z_CKV = 6144
Z_KR = 6656
Z_AL = 6784
Z_W = 6912

TM = 832
TR = 416
TY = 512
TN_IN = 768
TN_OUT = 512
TN_FF = 256
TK_DOWN = D_FF // 2
TN_DOWN = 512
TN_PLE = 512
TQ = 640
ATT_T = 512
ATT_HB = 2
Q_SCALE = MLA_QK ** -0.5 * 1.4426950408889634
CONV_T = 128
GLA_CPS = 4
GLA_CP = 64
XPOSE = 128


def _params(*sem):
    return pltpu.CompilerParams(dimension_semantics=sem, vmem_limit_bytes=VMEM_LIMIT)


def _rms(x, g):
    return x * lax.rsqrt(jnp.mean(x * x, axis=-1, keepdims=True) + EPS) * g


def _sigmoid(x):
    return 1.0 / (1.0 + jnp.exp(-x))


def _dot(a, b):
    return jnp.dot(a, b, preferred_element_type=F32)


def _dot_nt(a, b):
    return lax.dot_general(a, b, (((1,), (1,)), ((), ())), preferred_element_type=F32)


def _fold_lanes(sq):
    part = sq[:, :LANES]
    for c in range(LANES, sq.shape[1], LANES):
        part = part + sq[:, c:c + LANES]
    return part


def _emit_norm_inputs(x_new, g_ref, xg_ref, ss_ref, first):
    xg_ref[...] = (x_new * g_ref[...]).astype(BF16)
    part = _fold_lanes(x_new * x_new)

    @pl.when(first)
    def _():
        ss_ref[...] = part

    @pl.when(jnp.logical_not(first))
    def _():
        ss_ref[...] += part


def _row_scale(ss_ref):
    return lax.rsqrt(jnp.sum(ss_ref[...], axis=-1, keepdims=True) * (1.0 / D_MODEL) + EPS)


def _norm_specs(imap_tile, imap_row, tn):
    return (pl.BlockSpec((1, tn), lambda *a: (0, imap_tile(*a)[1])),
            [pl.BlockSpec((TM, tn), imap_tile), pl.BlockSpec((TM, LANES), imap_row)],
            [jax.ShapeDtypeStruct((ROWS, D_MODEL), BF16), jax.ShapeDtypeStruct((ROWS, LANES), F32)])


def _norm_prep_kernel(x_ref, g_ref, xg_ref, ss_ref):
    x = x_ref[...]
    xg_ref[...] = (x * g_ref[...]).astype(BF16)
    ss_ref[...] = _fold_lanes(x * x)


def norm_prep(x, g):
    rows, d = x.shape
    return pl.pallas_call(
        _norm_prep_kernel,
        grid=(rows // TR,),
        in_specs=[pl.BlockSpec((TR, d), lambda i: (i, 0)),
                  pl.BlockSpec((1, d), lambda i: (0, 0))],
        out_specs=[pl.BlockSpec((TR, d), lambda i: (i, 0)), pl.BlockSpec((TR, LANES), lambda i: (i, 0))],
        out_shape=[jax.ShapeDtypeStruct((rows, d), BF16), jax.ShapeDtypeStruct((rows, LANES), F32)],
        compiler_params=_params("parallel"),
        name="norm_prep",
    )(x, g.reshape(1, d))


def _rmsnorm_kernel(x_ref, g_ref, o_ref):
    o_ref[...] = _rms(x_ref[...], g_ref[...]).astype(o_ref.dtype)


def rmsnorm_rows(x, g, row_off, rows, tile):
    d = x.shape[1]
    first = row_off // tile
    return pl.pallas_call(
        _rmsnorm_kernel,
        grid=(rows // tile,),
        in_specs=[pl.BlockSpec((tile, d), lambda i: (first + i, 0)),
                  pl.BlockSpec((1, d), lambda i: (0, 0))],
        out_specs=pl.BlockSpec((tile, d), lambda i: (i, 0)),
        out_shape=jax.ShapeDtypeStruct((rows, d), F32),
        compiler_params=_params("parallel"),
        name="rmsnorm_rows",
    )(x, g.reshape(1, d))


def _in_proj_kernel(xg_ref, ss_ref, w_ref, o_ref):
    o_ref[...] = _dot(xg_ref[...], w_ref[...]) * _row_scale(ss_ref)


def in_proj(xg, ss, w, layer):
    rows, k = xg.shape
    n = w.shape[2]
    return pl.pallas_call(
        _in_proj_kernel,
        grid=(rows // TM, n // TN_IN),
        in_specs=[pl.BlockSpec((TM, k), lambda i, j: (i, 0)),
                  pl.BlockSpec((TM, LANES), lambda i, j: (i, 0)),
                  pl.BlockSpec((None, k, TN_IN), lambda i, j: (layer, 0, j))],
        out_specs=pl.BlockSpec((TM, TN_IN), lambda i, j: (i, j)),
        out_shape=jax.ShapeDtypeStruct((rows, n), F32),
        compiler_params=_params("parallel", "arbitrary"),
        name="in_proj",
    )(xg, ss, w)


def _out_proj_kernel(oa_ref, ob_ref, oc_ref, wa_ref, wb_ref, wc_ref, x_ref, g_ref, o_ref, xg_ref, ss_ref):
    acc = _dot(oa_ref[...], wa_ref[...])
    acc += _dot(ob_ref[...], wb_ref[...])
    acc += _dot(oc_ref[...], wc_ref[...])
    x_new = x_ref[...] + acc
    o_ref[...] = x_new
    _emit_norm_inputs(x_new, g_ref, xg_ref, ss_ref, pl.program_id(1) == 0)


def out_proj(o_a, o_b, o_c, w, x, g_next, layer):
    rows = x.shape[0]
    da, db, dc = o_a.shape[1], o_b.shape[1], o_c.shape[1]
    g_spec, n_specs, n_shapes = _norm_specs(lambda i, j: (i, j), lambda i, j: (i, 0), TN_OUT)
    return pl.pallas_call(
        _out_proj_kernel,
        grid=(rows // TM, D_MODEL // TN_OUT),
        in_specs=[pl.BlockSpec((TM, da), lambda i, j: (i, 0)),
                  pl.BlockSpec((TM, db), lambda i, j: (i, 0)),
                  pl.BlockSpec((TM, dc), lambda i, j: (i, 0)),
                  pl.BlockSpec((None, da, TN_OUT), lambda i, j: (layer, 0, j)),
                  pl.BlockSpec((None, db, TN_OUT), lambda i, j: (layer, da // db, j)),
                  pl.BlockSpec((None, dc, TN_OUT), lambda i, j: (layer, (da + db) // dc, j)),
                  pl.BlockSpec((TM, TN_OUT), lambda i, j: (i, j)),
                  g_spec],
        out_specs=[pl.BlockSpec((TM, TN_OUT), lambda i, j: (i, j))] + n_specs,
        out_shape=[jax.ShapeDtypeStruct((rows, D_MODEL), F32)] + n_shapes,
        compiler_params=_params("parallel", "arbitrary"),
        name="out_proj",
    )(o_a, o_b, o_c, w, w, w, x, g_next.reshape(1, -1))


def _ffn_gate_up_kernel(xg_ref, ss_ref, wg_ref, wu_ref, o_ref):
    xg = xg_ref[...]
    r = _row_scale(ss_ref)
    g = _dot(xg, wg_ref[...].astype(BF16)) * r
    u = _dot(xg, wu_ref[...].astype(BF16)) * r
    o_ref[...] = (g * _sigmoid(g) * u).astype(o_ref.dtype)


def ffn_gate_up(xg, ss, wg, wu, layer):
    rows, k = xg.shape
    return pl.pallas_call(
        _ffn_gate_up_kernel,
        grid=(rows // TM, D_FF // TN_FF),
        in_specs=[pl.BlockSpec((TM, k), lambda i, j: (i, 0)),
                  pl.BlockSpec((TM, LANES), lambda i, j: (i, 0)),
                  pl.BlockSpec((None, k, TN_FF), lambda i, j: (layer, 0, j)),
                  pl.BlockSpec((None, k, TN_FF), lambda i, j: (layer, 0, j))],
        out_specs=pl.BlockSpec((TM, TN_FF), lambda i, j: (i, j)),
        out_shape=jax.ShapeDtypeStruct((rows, D_FF), BF16),
        compiler_params=_params("parallel", "arbitrary"),
        name="ffn_gate_up",
    )(xg, ss, wg, wu)


def _ffn_down_kernel(a_ref, w_ref, x_ref, g_ref, o_ref, xg_ref, ss_ref):
    part = _dot(a_ref[...], w_ref[...])
    k = pl.program_id(2)

    @pl.when(k == 0)
    def _():
        o_ref[...] = x_ref[...] + part

    @pl.when(k != 0)
    def _():
        o_ref[...] += part

    @pl.when(k == pl.num_programs(2) - 1)
    def _():
        _emit_norm_inputs(o_ref[...], g_ref, xg_ref, ss_ref, pl.program_id(1) == 0)


def ffn_down(a, w, x, g_next, layer):
    rows = x.shape[0]
    g_spec, n_specs, n_shapes = _norm_specs(lambda i, j, k: (i, j), lambda i, j, k: (i, 0), TN_DOWN)
    return pl.pallas_call(
        _ffn_down_kernel,
        grid=(rows // TM, D_MODEL // TN_DOWN, D_FF // TK_DOWN),
        in_specs=[pl.BlockSpec((TM, TK_DOWN), lambda i, j, k: (i, k)),
                  pl.BlockSpec((None, TK_DOWN, TN_DOWN), lambda i, j, k: (layer, k, j)),
                  pl.BlockSpec((TM, TN_DOWN), lambda i, j, k: (i, j)),
                  g_spec],
        out_specs=[pl.BlockSpec((TM, TN_DOWN), lambda i, j, k: (i, j))] + n_specs,
        out_shape=[jax.ShapeDtypeStruct((rows, D_MODEL), F32)] + n_shapes,
        compiler_params=_params("parallel", "arbitrary", "arbitrary"),
        name="ffn_down",
    )(a, w, x, g_next.reshape(1, -1))


def _ple_kernel(xg_ref, ss_ref, p_ref, wg_ref, wp_ref, x_ref, *rest, emit):
    gate = _sigmoid(_dot(xg_ref[...], wg_ref[...]) * _row_scale(ss_ref))
    proj = _dot(p_ref[...].astype(BF16), wp_ref[...])
    x_new = x_ref[...] + gate * proj
    if emit:
        g_ref, o_ref, xg_out_ref, ss_out_ref = rest
        _emit_norm_inputs(x_new, g_ref, xg_out_ref, ss_out_ref, pl.program_id(1) == 0)
    else:
        (o_ref,) = rest
    o_ref[...] = x_new


def ple_update(xg, ss, p, wg, wp, x, g_next, layer):
    rows, k = xg.shape
    emit = g_next is not None
    g_spec, n_specs, n_shapes = _norm_specs(lambda i, j: (i, j), lambda i, j: (i, 0), TN_PLE)
    outs = pl.pallas_call(
        functools.partial(_ple_kernel, emit=emit),
        grid=(rows // TM, D_MODEL // TN_PLE),
        in_specs=[pl.BlockSpec((TM, k), lambda i, j: (i, 0)),
                  pl.BlockSpec((TM, LANES), lambda i, j: (i, 0)),
                  pl.BlockSpec((None, TM, D_PLE), lambda i, j: (layer, i, 0)),
                  pl.BlockSpec((None, k, TN_PLE), lambda i, j: (layer, 0, j)),
                  pl.BlockSpec((None, D_PLE, TN_PLE), lambda i, j: (layer, 0, j)),
                  pl.BlockSpec((TM, TN_PLE), lambda i, j: (i, j))] + ([g_spec] if emit else []),
        out_specs=[pl.BlockSpec((TM, TN_PLE), lambda i, j: (i, j))] + (n_specs if emit else []),
        out_shape=[jax.ShapeDtypeStruct((rows, D_MODEL), F32)] + (n_shapes if emit else []),
        compiler_params=_params("parallel", "arbitrary"),
        name="ple_update",
    )(xg, ss, p, wg, wp, x, *([g_next.reshape(1, -1)] if emit else []))
    return outs if emit else (outs[0], None, None)


def _swap_rope_halves(x):
    lane = lax.broadcasted_iota(jnp.int32, x.shape, 1)
    first = (lane % MLA_ROPE) < (MLA_ROPE // 2)
    return jnp.where(first, pltpu.roll(x, LANES - MLA_ROPE // 2, 1), pltpu.roll(x, MLA_ROPE // 2, 1))


def _mla_q_kernel(z_ref, g_ref, wt_ref, cos_ref, sin_ref, qt_ref):
    cq = _rms(z_ref[...], g_ref[...]).astype(BF16)
    qt = _dot_nt(wt_ref[...], cq)
    cos, sin = cos_ref[...], sin_ref[...]
    half = MLA_ROPE // 2
    for h in range(MLA_HEADS):
        base = h * MLA_QK
        x1 = qt[base + MLA_NOPE:base + MLA_NOPE + half]
        x2 = qt[base + MLA_NOPE + half:base + MLA_QK]
        qt_ref[h, :MLA_NOPE, :] = (qt[base:base + MLA_NOPE] * Q_SCALE).astype(BF16)
        qt_ref[h, MLA_NOPE:MLA_NOPE + half, :] = ((x1 * cos - x2 * sin) * Q_SCALE).astype(BF16)
        qt_ref[h, MLA_NOPE + half:, :] = ((x2 * cos + x1 * sin) * Q_SCALE).astype(BF16)


def mla_q(z, g, w_uq_t, cos_t, sin_t):
    rows = z.shape[0]
    tab_spec = pl.BlockSpec((MLA_ROPE // 2, TQ), lambda i: (0, i))
    return pl.pallas_call(
        _mla_q_kernel,
        grid=(rows // TQ,),
        in_specs=[pl.BlockSpec((TQ, MLA_Q_RANK), lambda i: (i, Z_CQ // MLA_Q_RANK)),
                  pl.BlockSpec((1, MLA_Q_RANK), lambda i: (0, 0)),
                  pl.BlockSpec(w_uq_t.shape, lambda i: (0, 0)),
                  tab_spec, tab_spec],
        out_specs=pl.BlockSpec((MLA_HEADS, MLA_QK, TQ), lambda i: (0, 0, i)),
        out_shape=jax.ShapeDtypeStruct((MLA_HEADS, MLA_QK, rows), BF16),
        compiler_params=_params("parallel"),
        name="mla_q",
    )(z, g.reshape(1, -1), w_uq_t, cos_t, sin_t)


def _mla_latent_kernel(zc_ref, zk_ref, g_ref, ck_ref, sk_ref, ckv_ref, kr_ref):
    ckv_ref[...] = _rms(zc_ref[...], g_ref[...])
    kr = zk_ref[...]
    rot = kr * ck_ref[...] + _swap_rope_halves(kr) * sk_ref[...]
    kr_ref[...] = rot[:, :MLA_ROPE]


def mla_latent(z, g, ck, sk):
    rows = z.shape[0]
    return pl.pallas_call(
        _mla_latent_kernel,
        grid=(rows // TR,),
        in_specs=[pl.BlockSpec((TR, MLA_KV_RANK), lambda i: (i, Z_CKV // MLA_KV_RANK)),
                  pl.BlockSpec((TR, LANES), lambda i: (i, Z_KR // LANES)),
                  pl.BlockSpec((1, MLA_KV_RANK), lambda i: (0, 0)),
                  pl.BlockSpec((TR, LANES), lambda i: (i, 0)),
                  pl.BlockSpec((TR, LANES), lambda i: (i, 0))],
        out_specs=[pl.BlockSpec((TR, MLA_KV_RANK), lambda i: (i, 0)),
                   pl.BlockSpec((TR, MLA_ROPE), lambda i: (i, 0))],
        out_shape=[jax.ShapeDtypeStruct((rows, MLA_KV_RANK), F32),
                   jax.ShapeDtypeStruct((rows, MLA_ROPE), F32)],
        compiler_params=_params("parallel"),
        name="mla_latent",
    )(z, z, g.reshape(1, -1), ck, sk)


def _kv_proj_kernel(ckv_ref, kr_ref, wk_ref, wvt_ref, k_ref, vt_ref):
    c = ckv_ref[...].astype(BF16)
    kn = _dot(c, wk_ref[...])
    vt = _dot_nt(wvt_ref[...], c)
    kr = kr_ref[...].astype(BF16)
    for h in range(MLA_HEADS):
        k_ref[h, :, :MLA_NOPE] = kn[:, h * MLA_NOPE:(h + 1) * MLA_NOPE].astype(BF16)
        k_ref[h, :, MLA_NOPE:] = kr
        vt_ref[h, 0] = vt[h * MLA_V:(h + 1) * MLA_V].astype(BF16)


def kv_proj(ckv, kr, w_uk, w_uv_t, tile, rows):
    return pl.pallas_call(
        _kv_proj_kernel,
        grid=(rows // tile,),
        in_specs=[pl.BlockSpec((tile, MLA_KV_RANK), lambda i: (i, 0)),
                  pl.BlockSpec((tile, MLA_ROPE), lambda i: (i, 0)),
                  pl.BlockSpec(w_uk.shape, lambda i: (0, 0)),
                  pl.BlockSpec(w_uv_t.shape, lambda i: (0, 0))],
        out_specs=[pl.BlockSpec((MLA_HEADS, tile, MLA_QK), lambda i: (0, i, 0)),
                   pl.BlockSpec((MLA_HEADS, 1, MLA_V, tile), lambda i: (0, i, 0, 0))],
        out_shape=[jax.ShapeDtypeStruct((MLA_HEADS, rows, MLA_QK), BF16),
                   jax.ShapeDtypeStruct((MLA_HEADS, rows // tile, MLA_V, tile), BF16)],
        compiler_params=_params("parallel"),
        name="kv_proj",
    )(ckv, kr, w_uk, w_uv_t)


def _attn_prompt_kernel(qt_ref, k_ref, vt_ref, o_ref, sa_sc, sb_sc, m_sc, l_sc, acc_sc):
    i = pl.program_id(1)
    m_sc[...] = jnp.full(m_sc.shape, -jnp.inf, F32)
    l_sc[...] = jnp.zeros(l_sc.shape, F32)
    acc_sc[...] = jnp.zeros(acc_sc.shape, F32)

    def scores(j, s_sc):
        start = pl.multiple_of(j * ATT_T, ATT_T)
        for hb in range(ATT_HB):
            s_sc[hb] = _dot(k_ref[hb, pl.ds(start, ATT_T), :], qt_ref[hb])

    def step(j, s_sc, masked):
        for hb in range(ATT_HB):
            s = s_sc[hb]
            if masked:
                kc = lax.broadcasted_iota(jnp.int32, s.shape, 0) // CHUNK
                qc = lax.broadcasted_iota(jnp.int32, s.shape, 1) // CHUNK
                s = jnp.where(kc <= qc, s, -jnp.inf)
            m_prev = m_sc[hb]
            m_new = jnp.maximum(m_prev, jnp.max(s, axis=0, keepdims=True))
            alpha = jnp.exp2(m_prev - m_new)
            p = jnp.exp2(s - m_new)
            l_sc[hb] = alpha * l_sc[hb] + jnp.sum(p, axis=0, keepdims=True)
            acc_sc[hb] = alpha * acc_sc[hb] + _dot(vt_ref[hb, j], p.astype(BF16))
            m_sc[hb] = m_new

    def pair(jj, carry):
        j = 2 * jj
        scores(j + 1, sb_sc)
        step(j, sa_sc, False)
        scores(j + 2, sa_sc)
        step(j + 1, sb_sc, False)
        return carry

    scores(0, sa_sc)
    lax.fori_loop(0, i // 2, pair, 0)

    @pl.when(i % 2 == 0)
    def _():
        step(i, sa_sc, True)

    @pl.when(i % 2 == 1)
    def _():
        scores(i, sb_sc)
        step(i - 1, sa_sc, False)
        step(i, sb_sc, True)

    for hb in range(ATT_HB):
        out_t = acc_sc[hb] / l_sc[hb]
        o_ref[:, hb * MLA_V:(hb + 1) * MLA_V] = out_t.T.astype(o_ref.dtype)


def attn_prompt(qt, k, vt):
    nblk = SEQ // ATT_T
    return pl.pallas_call(
        _attn_prompt_kernel,
        grid=(MLA_HEADS // ATT_HB, nblk),
        in_specs=[pl.BlockSpec((ATT_HB, MLA_QK, ATT_T), lambda g, i: (g, 0, i)),
                  pl.BlockSpec((ATT_HB, SEQ, MLA_QK), lambda g, i: (g, 0, 0)),
                  pl.BlockSpec((ATT_HB, nblk, MLA_V, ATT_T), lambda g, i: (g, 0, 0, 0))],
        out_specs=pl.BlockSpec((ATT_T, ATT_HB * MLA_V), lambda g, i: (i, g)),
        out_shape=jax.ShapeDtypeStruct((ROWS, MLA_HEADS * MLA_V), BF16),
        scratch_shapes=[pltpu.VMEM((ATT_HB, ATT_T, ATT_T), F32), pltpu.VMEM((ATT_HB, ATT_T, ATT_T), F32),
                        pltpu.VMEM((ATT_HB, 1, ATT_T), F32), pltpu.VMEM((ATT_HB, 1, ATT_T), F32),
                        pltpu.VMEM((ATT_HB, MLA_V, ATT_T), F32)],
        compiler_params=_params("parallel", "arbitrary"),
        name="attn_prompt",
    )(qt, k, vt)


def _attn_sample_kernel(qt_ref, ckv_ref, kr_ref, wk_ref, wv_ref, into_ref, o_ref, qlat_sc, qrope_sc, olat_sc):
    del into_ref
    b = pl.program_id(0)
    pad_rope = jnp.zeros((LANES - MLA_ROPE, N_SAMPLE), F32)

    @pl.when(b == 0)
    def _():
        for h in range(MLA_HEADS):
            q_nope = qt_ref[h, :MLA_NOPE, :].astype(F32).T.astype(BF16)
            qlat_sc[h] = _dot_nt(q_nope, wk_ref[:, h * MLA_NOPE:(h + 1) * MLA_NOPE])
            q_rope = jnp.concatenate([qt_ref[h, MLA_NOPE:, :].astype(F32), pad_rope], axis=0)
            qrope_sc[h] = q_rope.T

    rows = pl.ds(pl.multiple_of(b * DEC_SEQ, DEC_SEQ), DEC_SEQ)
    q_lat = jnp.concatenate([qlat_sc[h, rows, :] for h in range(MLA_HEADS)], axis=0)
    q_rope = jnp.concatenate([qrope_sc[h, rows, :] for h in range(MLA_HEADS)], axis=0)[:, :MLA_ROPE]
    c = ckv_ref[0].astype(BF16)
    s = _dot_nt(q_lat.astype(BF16), c) + _dot_nt(q_rope.astype(BF16), kr_ref[0].astype(BF16))
    kpos = lax.broadcasted_iota(jnp.int32, s.shape, 1)
    qpos = PAST_LEN + lax.broadcasted_iota(jnp.int32, s.shape, 0) % DEC_SEQ
    kchunk = jnp.where(kpos < SAMPLE_KEYS, kpos // CHUNK, SAMPLE_KEYS_PAD)
    s = jnp.where(kchunk <= qpos // CHUNK, s, -jnp.inf)
    p = jnp.exp2(s - jnp.max(s, axis=-1, keepdims=True))
    o_lat = _dot(p.astype(BF16), c) / jnp.sum(p, axis=-1, keepdims=True)
    for h in range(MLA_HEADS):
        olat_sc[h, rows, :] = o_lat[h * DEC_SEQ:(h + 1) * DEC_SEQ]

    @pl.when(b == DEC_BATCH - 1)
    def _():
        for h in range(MLA_HEADS):
            o_h = _dot(olat_sc[h].astype(BF16), wv_ref[:, h * MLA_V:(h + 1) * MLA_V])
            o_ref[:, h * MLA_V:(h + 1) * MLA_V] = o_h.astype(o_ref.dtype)


def attn_sample(qt, ckv, kr, w_uk, w_uv, into):
    return pl.pallas_call(
        _attn_sample_kernel,
        grid=(DEC_BATCH,),
        in_specs=[pl.BlockSpec((MLA_HEADS, MLA_QK, N_SAMPLE), lambda b: (0, 0, SEQ // N_SAMPLE)),
                  pl.BlockSpec((1, SAMPLE_KEYS_PAD, MLA_KV_RANK), lambda b: (b, 0, 0)),
                  pl.BlockSpec((1, SAMPLE_KEYS_PAD, MLA_ROPE), lambda b: (b, 0, 0)),
                  pl.BlockSpec(w_uk.shape, lambda b: (0, 0)),
                  pl.BlockSpec(w_uv.shape, lambda b: (0, 0)),
                  pl.BlockSpec(memory_space=pl.ANY)],
        out_specs=pl.BlockSpec((N_SAMPLE, MLA_HEADS * MLA_V), lambda b: (SEQ // N_SAMPLE, 0)),
        out_shape=jax.ShapeDtypeStruct(into.shape, into.dtype),
        input_output_aliases={5: 0},
        scratch_shapes=[pltpu.VMEM((MLA_HEADS, N_SAMPLE, MLA_KV_RANK), F32),
                        pltpu.VMEM((MLA_HEADS, N_SAMPLE, LANES), F32),
                        pltpu.VMEM((MLA_HEADS, N_SAMPLE, MLA_KV_RANK), F32)],
        compiler_params=_params("arbitrary"),
        name="attn_sample",
    )(qt, ckv, kr, w_uk, w_uv, into)


CONV_PAD = 32
CONV_OFF = CONV_PAD - CONV_HIST


def _conv_kernel(za_ref, zb_ref, past_ref, w_ref, b_ref, g_ref, beta_ref, *rest, t_rows):
    o_ref, hist_ref, ext_sc = rest[-3:]

    @pl.when(pl.program_id(1) == 0)
    def _():
        ext_sc[0:SUBLANES, :] = jnp.zeros((SUBLANES, CONV_CH), F32)
        ext_sc[CONV_OFF:CONV_PAD, :] = past_ref[0]

    u = za_ref[...] * _sigmoid(zb_ref[...])
    ext_sc[CONV_PAD:CONV_PAD + t_rows, :] = u
    cols = []
    for c0 in range(0, CONV_CH, LANES):
        acc = None
        for shift in range(SUBLANES):
            win = t_rows + (SUBLANES if shift else 0)
            group = None
            for base in range(0, CONV_PAD + SUBLANES, SUBLANES):
                tap = base + shift - CONV_OFF
                if 0 <= tap < CONV_WIDTH:
                    term = w_ref[tap:tap + 1, c0:c0 + LANES] * ext_sc[base:base + win, c0:c0 + LANES]
                    group = term if group is None else group + term
            part = group[shift:shift + t_rows]
            acc = part if acc is None else acc + part
        cols.append(acc)
    y = jnp.concatenate(cols, axis=1) + b_ref[...]
    yc = y - jnp.mean(y, axis=-1, keepdims=True)
    y = yc * lax.rsqrt(jnp.mean(yc * yc, axis=-1, keepdims=True) + EPS) * g_ref[...] + beta_ref[...]
    o_ref[...] = (y * _sigmoid(y)).astype(o_ref.dtype)
    tail = ext_sc[t_rows + CONV_OFF:t_rows + CONV_PAD, :]
    hist_ref[0] = tail
    ext_sc[CONV_OFF:CONV_PAD, :] = tail


def _into_args(into):
    if into is None:
        return [], [], None
    return [pl.BlockSpec(memory_space=pl.ANY)], [into], into


def conv_mixer(z, past, w, b, g, beta, *, batch, seq, row_off, t_rows, into=None):
    steps = seq // t_rows
    first = row_off // t_rows

    def rows(bi, t):
        return first + bi * steps + t

    vec = pl.BlockSpec((1, CONV_CH), lambda bi, t: (0, 0))
    extra_specs, extra_args, alias = _into_args(into)
    return pl.pallas_call(
        functools.partial(_conv_kernel, t_rows=t_rows),
        grid=(batch, steps),
        in_specs=[pl.BlockSpec((t_rows, CONV_CH), lambda bi, t: (rows(bi, t), Z_CA // CONV_CH)),
                  pl.BlockSpec((t_rows, CONV_CH), lambda bi, t: (rows(bi, t), Z_CB // CONV_CH)),
                  pl.BlockSpec((1, CONV_HIST, CONV_CH), lambda bi, t: (bi, 0, 0)),
                  pl.BlockSpec((CONV_WIDTH, CONV_CH), lambda bi, t: (0, 0)),
                  vec, vec, vec] + extra_specs,
        out_specs=[pl.BlockSpec((t_rows, CONV_CH), lambda bi, t: (rows(bi, t), 0)),
                   pl.BlockSpec((1, CONV_HIST, CONV_CH), lambda bi, t: (bi, 0, 0))],
        out_shape=[jax.ShapeDtypeStruct((ROWS, CONV_CH), BF16),
                   jax.ShapeDtypeStruct((batch, CONV_HIST, CONV_CH), F32)],
        input_output_aliases={} if alias is None else {7: 0},
        scratch_shapes=[pltpu.VMEM((CONV_PAD + t_rows, CONV_CH), F32)],
        compiler_params=_params("parallel", "arbitrary"),
        name="conv_mixer",
    )(z, z, past, w, b.reshape(1, -1), g.reshape(1, -1), beta.reshape(1, -1), *extra_args)


def _pad_rows(x, rows):
    if x.shape[0] == rows:
        return x
    return jnp.concatenate([x, jnp.zeros((rows - x.shape[0], x.shape[1]), x.dtype)], axis=0)


def _log_sigmoid(x):
    return jnp.minimum(x, 0.0) - jnp.log1p(jnp.exp(-jnp.abs(x)))


def _gla_kernel(q_ref, k_ref, v_ref, r_ref, al_ref, wa_ref, ba_ref, gn_ref, s0_ref, *rest, c, cps):
    o_ref, s_out_ref, s_sc = rest[-3:]
    step = pl.program_id(1)

    @pl.when(step == 0)
    def _():
        s_sc[...] = s0_ref[0]

    row = lax.broadcasted_iota(jnp.int32, (GLA_CP, GLA_CP), 0)
    col = lax.broadcasted_iota(jnp.int32, (GLA_CP, GLA_CP), 1)
    causal = col <= row
    rows_p = cps * GLA_CP
    gate = _dot(al_ref[...].astype(BF16), wa_ref[...]) + ba_ref[...]
    log_a = _pad_rows(_log_sigmoid(gate) / GLA_TAU, rows_p)
    row_p = lax.broadcasted_iota(jnp.int32, (rows_p, rows_p), 0)
    col_p = lax.broadcasted_iota(jnp.int32, (rows_p, rows_p), 1)
    tri = jnp.where(col_p <= row_p, jnp.where(row_p // GLA_CP == col_p // GLA_CP, 1.0, 0.0), 0.0)
    bcum_all = jnp.dot(tri, log_a, precision=lax.Precision.HIGHEST, preferred_element_type=F32)
    grow_all = jnp.exp(bcum_all)
    shrink_all = jnp.exp(-bcum_all)
    for ci in range(cps):
        r0 = ci * c
        p0 = ci * GLA_CP
        for h in range(GLA_HEADS):
            ks = slice(h * GLA_DK, (h + 1) * GLA_DK)
            vs = slice(h * GLA_DV, (h + 1) * GLA_DV)
            b = bcum_all[p0:p0 + GLA_CP, ks]
            q = _pad_rows(q_ref[r0:r0 + c, ks], GLA_CP) * GLA_DK ** -0.5
            k = _pad_rows(k_ref[r0:r0 + c, ks], GLA_CP)
            v = _pad_rows(v_ref[r0:r0 + c, vs], GLA_CP).astype(BF16)
            q_t = (q * grow_all[p0:p0 + GLA_CP, ks]).astype(BF16)
            k_t = (k * shrink_all[p0:p0 + GLA_CP, ks]).astype(BF16)
            a = jnp.where(causal, _dot_nt(q_t, k_t), 0.0)
            s_prev = s_sc[h]
            o = _dot(q_t, s_prev.astype(BF16)) + _dot(a.astype(BF16), v)
            b_last = b[c - 1:c, :]
            k_dec = k * jnp.exp(b_last - b)
            k_dec_t = _pad_rows(k_dec, XPOSE).T
            decay = jnp.exp(_pad_rows(b, XPOSE).T[:, c - 1:c])
            s_sc[h] = decay * s_prev + _dot(k_dec_t.astype(BF16), _pad_rows(v, XPOSE))
            on = _rms(o[:c], gn_ref[...])
            r = r_ref[r0:r0 + c, vs]
            o_ref[r0:r0 + c, vs] = (on * (r * _sigmoid(r))).astype(o_ref.dtype)

    @pl.when(step == pl.num_programs(1) - 1)
    def _():
        s_out_ref[0] = s_sc[...]


def gla_mixer(z, s0, wa2, ba, gn, *, batch, seq, row_off, c, cps, into=None):
    t_rows = c * cps
    steps = seq // t_rows
    first = row_off // t_rows

    def rows(bi, t):
        return first + bi * steps + t

    state = pl.BlockSpec((1, GLA_HEADS, GLA_DK, GLA_DV), lambda bi, t: (bi, 0, 0, 0))
    extra_specs, extra_args, alias = _into_args(into)
    return pl.pallas_call(
        functools.partial(_gla_kernel, c=c, cps=cps),
        grid=(batch, steps),
        in_specs=[pl.BlockSpec((t_rows, HK), lambda bi, t: (rows(bi, t), Z_GQ // HK)),
                  pl.BlockSpec((t_rows, HK), lambda bi, t: (rows(bi, t), Z_GK // HK)),
                  pl.BlockSpec((t_rows, HV), lambda bi, t: (rows(bi, t), Z_GV // HV)),
                  pl.BlockSpec((t_rows, HV), lambda bi, t: (rows(bi, t), Z_GR // HV)),
                  pl.BlockSpec((t_rows, LANES), lambda bi, t: (rows(bi, t), Z_AL // LANES)),
                  pl.BlockSpec((LANES, HK), lambda bi, t: (0, 0)),
                  pl.BlockSpec((1, HK), lambda bi, t: (0, 0)),
                  pl.BlockSpec((1, GLA_DV), lambda bi, t: (0, 0)),
                  state] + extra_specs,
        out_specs=[pl.BlockSpec((t_rows, HV), lambda bi, t: (rows(bi, t), 0)), state],
        out_shape=[jax.ShapeDtypeStruct((ROWS, HV), BF16),
                   jax.ShapeDtypeStruct((batch, GLA_HEADS, GLA_DK, GLA_DV), F32)],
        input_output_aliases={} if alias is None else {9: 0},
        scratch_shapes=[pltpu.VMEM((GLA_HEADS, GLA_DK, GLA_DV), F32)],
        compiler_params=_params("parallel", "arbitrary"),
        name="gla_mixer",
    )(z, z, z, z, z, wa2, ba.reshape(1, -1), gn.reshape(1, -1), s0, *extra_args)


_SRC_CONV = MLA_Q_RANK + MLA_KV_RANK + MLA_ROPE
_SRC_GLA = _SRC_CONV + 2 * CONV_CH
D_IN = _SRC_GLA + 2 * HK + 2 * HV + GLA_GATE_RANK
_W_IN_PIECES = (
    (_SRC_CONV, CONV_CH, Z_CA), (_SRC_CONV + CONV_CH, CONV_CH, Z_CB),
    (_SRC_GLA + 2 * HK, HV, Z_GV), (_SRC_GLA + 2 * HK + HV, HV, Z_GR),
    (0, MLA_Q_RANK, Z_CQ),
    (_SRC_GLA, HK, Z_GQ), (_SRC_GLA + HK, HK, Z_GK),
    (MLA_Q_RANK, MLA_KV_RANK, Z_CKV),
    (MLA_Q_RANK + MLA_KV_RANK, MLA_ROPE, Z_KR),
    (_SRC_GLA + 2 * HK + 2 * HV, GLA_GATE_RANK, Z_AL),
)
PACK_SRC = 64


def _pack_tables():
    first, valid = [], []
    for col in range(0, Z_W, LANES):
        src, width, dst = next(p for p in _W_IN_PIECES if p[2] <= col < p[2] + max(p[1], LANES))
        first.append((src + col - dst) // PACK_SRC)
        valid.append(min(LANES, width))
    return jnp.asarray(first, jnp.int32), jnp.asarray(valid, jnp.int32)


def _pack_w_in_kernel(first_ref, valid_ref, lo_ref, hi_ref, o_ref):
    del first_ref
    x = jnp.concatenate([lo_ref[...], hi_ref[...]], axis=0)
    row = lax.broadcasted_iota(jnp.int32, x.shape, 0)
    x = jnp.where(row < valid_ref[pl.program_id(1)], x, 0.0)
    o_ref[...] = x.T.astype(BF16)


def pack_w_in(w_in_t):
    first, valid = _pack_tables()
    last = (D_IN - 1) // PACK_SRC
    return pl.pallas_call(
        _pack_w_in_kernel,
        grid_spec=pltpu.PrefetchScalarGridSpec(
            num_scalar_prefetch=2,
            grid=(DEPTH, Z_W // LANES),
            in_specs=[pl.BlockSpec((None, PACK_SRC, D_MODEL), lambda l, c, first, valid: (l, first[c], 0)),
                      pl.BlockSpec((None, PACK_SRC, D_MODEL),
                                   lambda l, c, first, valid: (l, jnp.minimum(first[c] + 1, last), 0))],
            out_specs=pl.BlockSpec((None, D_MODEL, LANES), lambda l, c, first, valid: (l, 0, c)),
        ),
        out_shape=jax.ShapeDtypeStruct((DEPTH, D_MODEL, Z_W), BF16),
        compiler_params=_params("parallel", "arbitrary"),
        name="pack_w_in",
    )(first, valid, w_in_t, w_in_t)


def _rope_tables():
    half = MLA_ROPE // 2
    pos = jnp.concatenate([jnp.arange(SEQ, dtype=jnp.int32),
                           jnp.tile(PAST_LEN + jnp.arange(DEC_SEQ, dtype=jnp.int32), DEC_BATCH)])
    inv_freq = ROPE_THETA ** (-jnp.arange(half, dtype=F32) / half)
    ang = pos.astype(F32)[:, None] * inv_freq[None, :]
    cos, sin = jnp.cos(ang), jnp.sin(ang)
    c64 = jnp.concatenate([cos, cos], axis=1)
    s64 = jnp.concatenate([-sin, sin], axis=1)
    zero = jnp.zeros_like(s64)
    key = (jnp.concatenate([c64, zero], axis=1), jnp.concatenate([s64, zero], axis=1))
    return (cos.T, sin.T), key


@jax.jit
def _forward(x_prompt, x_sample, cache_ckv, cache_krope, cache_conv, state_gla, p_prompt, p_sample, norm_mix,
             w_in, mla_q_norm, mla_kv_norm, mla_w_uq, mla_w_uk, mla_w_uv, conv_dw_w, conv_dw_b, conv_ln_g,
             conv_ln_b, gla_w_a2, gla_b_a, gla_norm, w_out, norm_ffn, ffn_w_gate, ffn_w_up, ffn_w_down,
             norm_ple, ple_w_gate, ple_w_proj, norm_final):
    w_in_p = pack_w_in(jnp.swapaxes(w_in, 1, 2))
    w_uq_t = mla_w_uq.astype(BF16).transpose(0, 2, 1)
    w_uk = mla_w_uk.reshape(DEPTH, MLA_KV_RANK, MLA_HEADS * MLA_NOPE).astype(BF16)
    w_uv = mla_w_uv.reshape(DEPTH, MLA_KV_RANK, MLA_HEADS * MLA_V).astype(BF16)
    w_uv_t = w_uv.transpose(0, 2, 1)
    w_a2 = jnp.concatenate([gla_w_a2.astype(BF16),
                            jnp.zeros((DEPTH, LANES - GLA_GATE_RANK, HK), BF16)], axis=1)
    w_out_b = w_out.astype(BF16)
    w_gate, w_up = ffn_w_gate, ffn_w_up
    w_down = ffn_w_down.astype(BF16)
    w_pg, w_pp = ple_w_gate.astype(BF16), ple_w_proj.astype(BF16)
    (cos_t, sin_t), (ck, sk) = _rope_tables()

    x = jnp.concatenate([x_prompt.reshape(SEQ, D_MODEL), x_sample.reshape(N_SAMPLE, D_MODEL)], axis=0)
    p_all = jnp.concatenate([p_prompt.reshape(DEPTH, SEQ, D_PLE), p_sample.reshape(DEPTH, N_SAMPLE, D_PLE)], axis=1)
    zero_conv = jnp.zeros((1, CONV_HIST, CONV_CH), F32)
    zero_gla = jnp.zeros((1, GLA_HEADS, GLA_DK, GLA_DV), F32)

    ckvs, krs, convs_p, convs_s, glas_p, glas_s = [], [], [], [], [], []
    xg, ss = norm_prep(x, norm_mix[0])
    for i in range(DEPTH):
        z = in_proj(xg, ss, w_in_p, i)

        qt = mla_q(z, mla_q_norm[i], w_uq_t[i], cos_t, sin_t)
        ckv, kr = mla_latent(z, mla_kv_norm[i], ck, sk)
        k_p, vt_p = kv_proj(ckv, kr, w_uk[i], w_uv_t[i], ATT_T, SEQ)
        pad = SAMPLE_KEYS_PAD - SAMPLE_KEYS
        ckv_s = jnp.concatenate([cache_ckv[i], ckv[SEQ:].reshape(DEC_BATCH, DEC_SEQ, MLA_KV_RANK),
                                 jnp.zeros((DEC_BATCH, pad, MLA_KV_RANK), F32)], axis=1)
        kr_s = jnp.concatenate([cache_krope[i], kr[SEQ:].reshape(DEC_BATCH, DEC_SEQ, MLA_ROPE),
                                jnp.zeros((DEC_BATCH, pad, MLA_ROPE), F32)], axis=1)
        o_a = attn_sample(qt, ckv_s, kr_s, w_uk[i], w_uv[i], into=attn_prompt(qt, k_p, vt_p))

        conv_args = (conv_dw_w[i], conv_dw_b[i], conv_ln_g[i], conv_ln_b[i])
        o_b, conv_p = conv_mixer(z, zero_conv, *conv_args, batch=1, seq=SEQ, row_off=0, t_rows=CONV_T)
        o_b, conv_s = conv_mixer(z, cache_conv[i], *conv_args, batch=DEC_BATCH, seq=DEC_SEQ, row_off=SEQ,
                                 t_rows=DEC_SEQ, into=o_b)

        gla_args = (w_a2[i], gla_b_a[i], gla_norm[i])
        o_c, gla_p = gla_mixer(z, zero_gla, *gla_args, batch=1, seq=SEQ, row_off=0, c=CHUNK, cps=GLA_CPS)
        o_c, gla_s = gla_mixer(z, state_gla[i], *gla_args, batch=DEC_BATCH, seq=DEC_SEQ, row_off=SEQ,
                               c=DEC_SEQ, cps=1, into=o_c)

        x, xg, ss = out_proj(o_a, o_b, o_c, w_out_b, x, norm_ffn[i], i)
        x, xg, ss = ffn_down(ffn_gate_up(xg, ss, w_gate, w_up, i), w_down, x, norm_ple[i], i)
        g_next = norm_mix[i + 1] if i + 1 < DEPTH else None
        x, xg, ss = ple_update(xg, ss, p_all, w_pg, w_pp, x, g_next, i)

        ckvs.append(ckv)
        krs.append(kr)
        convs_p.append(conv_p)
        convs_s.append(conv_s)
        glas_p.append(gla_p)
        glas_s.append(gla_s)

    y_prompt = rmsnorm_rows(x, norm_final, 0, SEQ, TY)
    y_sample = rmsnorm_rows(x, norm_final, SEQ, N_SAMPLE, N_SAMPLE)
    ckv_all, kr_all = jnp.stack(ckvs), jnp.stack(krs)
    return (y_prompt.reshape(1, SEQ, D_MODEL),
            y_sample.reshape(DEC_BATCH, DEC_SEQ, D_MODEL),
            ckv_all[:, :SEQ].reshape(DEPTH, 1, SEQ, MLA_KV_RANK),
            kr_all[:, :SEQ].reshape(DEPTH, 1, SEQ, MLA_ROPE),
            jnp.stack(convs_p),
            jnp.stack(glas_p),
            ckv_all[:, SEQ:].reshape(DEPTH, DEC_BATCH, DEC_SEQ, MLA_KV_RANK),
            kr_all[:, SEQ:].reshape(DEPTH, DEC_BATCH, DEC_SEQ, MLA_ROPE),
            jnp.stack(convs_s),
            jnp.stack(glas_s))


def kernel(x_prompt, x_sample, cache_ckv, cache_krope, cache_conv, state_gla, p_prompt, p_sample, norm_mix, w_in, mla_q_norm, mla_kv_norm, mla_w_uq, mla_w_uk, mla_w_uv, conv_dw_w, conv_dw_b, conv_ln_g, conv_ln_b, gla_w_a2, gla_b_a, gla_norm, w_out, norm_ffn, ffn_w_gate, ffn_w_up, ffn_w_down, norm_ple, ple_w_gate, ple_w_proj, norm_final):
    return _forward(x_prompt, x_sample, cache_ckv, cache_krope, cache_conv, state_gla, p_prompt, p_sample,
                    norm_mix, w_in, mla_q_norm, mla_kv_norm, mla_w_uq, mla_w_uk, mla_w_uv, conv_dw_w, conv_dw_b,
                    conv_ln_g, conv_ln_b, gla_w_a2, gla_b_a, gla_norm, w_out, norm_ffn, ffn_w_gate, ffn_w_up,
                    ffn_w_down, norm_ple, ple_w_gate, ple_w_proj, norm_final)
```

```python
import functools

import jax
import jax.numpy as jnp
from jax import lax
from jax.experimental import pallas as pl
from jax.experimental.pallas import tpu as pltpu

F32 = jnp.float32
BF16 = jnp.bfloat16

D_MODEL = 4096
SEQ = 8192
DEPTH = 4
DEC_BATCH = 8
DEC_SEQ = 16
PAST_LEN = 1024
CHUNK = 64
EPS = 1e-6
D_PLE = 256
D_FF = 11008
MLA_HEADS = 16
MLA_Q_RANK = 1024
MLA_KV_RANK = 512
MLA_NOPE = 128
MLA_ROPE = 64
MLA_V = 128
MLA_QK = MLA_NOPE + MLA_ROPE
ROPE_THETA = 10000.0
CONV_CH = 1024
CONV_WIDTH = 31
CONV_HIST = CONV_WIDTH - 1
GLA_HEADS = 4
GLA_DK = 128
GLA_DV = 256
GLA_GATE_RANK = 16
GLA_TAU = 16.0
HK = GLA_HEADS * GLA_DK
HV = GLA_HEADS * GLA_DV

N_SAMPLE = DEC_BATCH * DEC_SEQ
ROWS = SEQ + N_SAMPLE
SAMPLE_KEYS = PAST_LEN + DEC_SEQ
SAMPLE_KEYS_PAD = 1152

LANES = 128
SUBLANES = 8
VMEM_LIMIT = 52 * 1024 * 1024

Z_CA = 0
Z_CB = 1024
Z_GV = 2048
Z_GR = 3072
Z_CQ = 4096
Z_GQ = 5120
Z_GK = 5632
Z_CKV = 6144
Z_KR = 6656
Z_AL = 6784
Z_W = 6912

TM = 1040
TM_FF = 1664
TR = 416
TY = 512
TN_IN = 768
TN_OUT = 512
TN_FF = 256
TK_DOWN = D_FF // 2
TN_DOWN = 512
TN_PLE = 512
TQ = 640
ATT_T = 512
ATT_HB = 2
Q_SCALE = MLA_QK ** -0.5 * 1.4426950408889634
CONV_T = 256
GLA_CPS = 4
GLA_CP = 64
XPOSE = 128


def _params(*sem):
    return pltpu.CompilerParams(dimension_semantics=sem, vmem_limit_bytes=VMEM_LIMIT)


def _rms(x, g):
    return x * lax.rsqrt(jnp.mean(x * x, axis=-1, keepdims=True) + EPS) * g


def _sigmoid(x):
    return 1.0 / (1.0 + jnp.exp(-x))


def _dot(a, b):
    return jnp.dot(a, b, preferred_element_type=F32)


def _dot_nt(a, b):
    return lax.dot_general(a, b, (((1,), (1,)), ((), ())), preferred_element_type=F32)


def _fold_lanes(sq):
    part = sq[:, :LANES]
    for c in range(LANES, sq.shape[1], LANES):
        part = part + sq[:, c:c + LANES]
    return part


def _emit_norm_inputs(x_new, g_ref, xg_ref, ss_ref, first):
    xg_ref[...] = (x_new * g_ref[...]).astype(BF16)
    part = _fold_lanes(x_new * x_new)

    @pl.when(first)
    def _():
        ss_ref[...] = part

    @pl.when(jnp.logical_not(first))
    def _():
        ss_ref[...] += part


def _row_scale(ss_ref):
    return lax.rsqrt(jnp.sum(ss_ref[...], axis=-1, keepdims=True) * (1.0 / D_MODEL) + EPS)


def _norm_specs(imap_tile, imap_row, tn):
    return (pl.BlockSpec((1, tn), lambda *a: (0, imap_tile(*a)[1])),
            [pl.BlockSpec((TM, tn), imap_tile), pl.BlockSpec((TM, LANES), imap_row)],
            [jax.ShapeDtypeStruct((ROWS, D_MODEL), BF16), jax.ShapeDtypeStruct((ROWS, LANES), F32)])


def _norm_prep_kernel(x_ref, g_ref, xg_ref, ss_ref):
    x = x_ref[...]
    xg_ref[...] = (x * g_ref[...]).astype(BF16)
    ss_ref[...] = _fold_lanes(x * x)


def norm_prep(x, g):
    rows, d = x.shape
    return pl.pallas_call(
        _norm_prep_kernel,
        grid=(rows // TR,),
        in_specs=[pl.BlockSpec((TR, d), lambda i: (i, 0)),
                  pl.BlockSpec((1, d), lambda i: (0, 0))],
        out_specs=[pl.BlockSpec((TR, d), lambda i: (i, 0)), pl.BlockSpec((TR, LANES), lambda i: (i, 0))],
        out_shape=[jax.ShapeDtypeStruct((rows, d), BF16), jax.ShapeDtypeStruct((rows, LANES), F32)],
        compiler_params=_params("parallel"),
        name="norm_prep",
    )(x, g.reshape(1, d))


def _rmsnorm_kernel(x_ref, g_ref, o_ref):
    o_ref[...] = _rms(x_ref[...], g_ref[...]).astype(o_ref.dtype)


def rmsnorm_rows(x, g, row_off, rows, tile):
    d = x.shape[1]
    first = row_off // tile
    return pl.pallas_call(
        _rmsnorm_kernel,
        grid=(rows // tile,),
        in_specs=[pl.BlockSpec((tile, d), lambda i: (first + i, 0)),
                  pl.BlockSpec((1, d), lambda i: (0, 0))],
        out_specs=pl.BlockSpec((tile, d), lambda i: (i, 0)),
        out_shape=jax.ShapeDtypeStruct((rows, d), F32),
        compiler_params=_params("parallel"),
        name="rmsnorm_rows",
    )(x, g.reshape(1, d))


def _in_proj_kernel(xg_ref, ss_ref, w_ref, o_ref):
    o_ref[...] = _dot(xg_ref[...], w_ref[...]) * _row_scale(ss_ref)


def in_proj(xg, ss, w, layer):
    rows, k = xg.shape
    n = w.shape[2]
    return pl.pallas_call(
        _in_proj_kernel,
        grid=(rows // TM, n // TN_IN),
        in_specs=[pl.BlockSpec((TM, k), lambda i, j: (i, 0)),
                  pl.BlockSpec((TM, LANES), lambda i, j: (i, 0)),
                  pl.BlockSpec((None, k, TN_IN), lambda i, j: (layer, 0, j))],
        out_specs=pl.BlockSpec((TM, TN_IN), lambda i, j: (i, j)),
        out_shape=jax.ShapeDtypeStruct((rows, n), F32),
        compiler_params=_params("parallel", "arbitrary"),
        name="in_proj",
    )(xg, ss, w)


def _out_proj_kernel(oa_ref, ob_ref, oc_ref, wa_ref, wb_ref, wc_ref, x_ref, g_ref, o_ref, xg_ref, ss_ref):
    acc = _dot(oa_ref[...], wa_ref[...])
    acc += _dot(ob_ref[...], wb_ref[...])
    acc += _dot(oc_ref[...], wc_ref[...])
    x_new = x_ref[...] + acc
    o_ref[...] = x_new
    _emit_norm_inputs(x_new, g_ref, xg_ref, ss_ref, pl.program_id(1) == 0)


def out_proj(o_a, o_b, o_c, w, x, g_next, layer):
    rows = x.shape[0]
    da, db, dc = o_a.shape[1], o_b.shape[1], o_c.shape[1]
    g_spec, n_specs, n_shapes = _norm_specs(lambda i, j: (i, j), lambda i, j: (i, 0), TN_OUT)
    return pl.pallas_call(
        _out_proj_kernel,
        grid=(rows // TM, D_MODEL // TN_OUT),
        in_specs=[pl.BlockSpec((TM, da), lambda i, j: (i, 0)),
                  pl.BlockSpec((TM, db), lambda i, j: (i, 0)),
                  pl.BlockSpec((TM, dc), lambda i, j: (i, 0)),
                  pl.BlockSpec((None, da, TN_OUT), lambda i, j: (layer, 0, j)),
                  pl.BlockSpec((None, db, TN_OUT), lambda i, j: (layer, da // db, j)),
                  pl.BlockSpec((None, dc, TN_OUT), lambda i, j: (layer, (da + db) // dc, j)),
                  pl.BlockSpec((TM, TN_OUT), lambda i, j: (i, j)),
                  g_spec],
        out_specs=[pl.BlockSpec((TM, TN_OUT), lambda i, j: (i, j))] + n_specs,
        out_shape=[jax.ShapeDtypeStruct((rows, D_MODEL), F32)] + n_shapes,
        compiler_params=_params("parallel", "arbitrary"),
        name="out_proj",
    )(o_a, o_b, o_c, w, w, w, x, g_next.reshape(1, -1))


def _ffn_gate_up_kernel(xg_ref, ss_ref, wg_ref, wu_ref, o_ref):
    xg = xg_ref[...]
    r = _row_scale(ss_ref)
    g = _dot(xg, wg_ref[...].astype(BF16)) * r
    u = _dot(xg, wu_ref[...].astype(BF16)) * r
    o_ref[...] = (g * _sigmoid(g) * u).astype(o_ref.dtype)


def ffn_gate_up(xg, ss, wg, wu, layer):
    rows, k = xg.shape
    return pl.pallas_call(
        _ffn_gate_up_kernel,
        grid=(rows // TM_FF, D_FF // TN_FF),
        in_specs=[pl.BlockSpec((TM_FF, k), lambda i, j: (i, 0)),
                  pl.BlockSpec((TM_FF, LANES), lambda i, j: (i, 0)),
                  pl.BlockSpec((None, k, TN_FF), lambda i, j: (layer, 0, j)),
                  pl.BlockSpec((None, k, TN_FF), lambda i, j: (layer, 0, j))],
        out_specs=pl.BlockSpec((TM_FF, TN_FF), lambda i, j: (i, j)),
        out_shape=jax.ShapeDtypeStruct((rows, D_FF), BF16),
        compiler_params=_params("parallel", "arbitrary"),
        name="ffn_gate_up",
    )(xg, ss, wg, wu)


def _ffn_down_kernel(a_ref, w_ref, x_ref, g_ref, o_ref, xg_ref, ss_ref):
    part = _dot(a_ref[...], w_ref[...])
    k = pl.program_id(2)

    @pl.when(k == 0)
    def _():
        o_ref[...] = x_ref[...] + part

    @pl.when(k != 0)
    def _():
        o_ref[...] += part

    @pl.when(k == pl.num_programs(2) - 1)
    def _():
        _emit_norm_inputs(o_ref[...], g_ref, xg_ref, ss_ref, pl.program_id(1) == 0)


def ffn_down(a, w, x, g_next, layer):
    rows = x.shape[0]
    g_spec, n_specs, n_shapes = _norm_specs(lambda i, j, k: (i, j), lambda i, j, k: (i, 0), TN_DOWN)
    return pl.pallas_call(
        _ffn_down_kernel,
        grid=(rows // TM, D_MODEL // TN_DOWN, D_FF // TK_DOWN),
        in_specs=[pl.BlockSpec((TM, TK_DOWN), lambda i, j, k: (i, k)),
                  pl.BlockSpec((None, TK_DOWN, TN_DOWN), lambda i, j, k: (layer, k, j)),
                  pl.BlockSpec((TM, TN_DOWN), lambda i, j, k: (i, j)),
                  g_spec],
        out_specs=[pl.BlockSpec((TM, TN_DOWN), lambda i, j, k: (i, j))] + n_specs,
        out_shape=[jax.ShapeDtypeStruct((rows, D_MODEL), F32)] + n_shapes,
        compiler_params=_params("parallel", "arbitrary", "arbitrary"),
        name="ffn_down",
    )(a, w, x, g_next.reshape(1, -1))


def _ple_kernel(xg_ref, ss_ref, p_ref, wg_ref, wp_ref, x_ref, *rest, emit):
    gate = _sigmoid(_dot(xg_ref[...], wg_ref[...]) * _row_scale(ss_ref))
    proj = _dot(p_ref[...].astype(BF16), wp_ref[...])
    x_new = x_ref[...] + gate * proj
    if emit:
        g_ref, o_ref, xg_out_ref, ss_out_ref = rest
        _emit_norm_inputs(x_new, g_ref, xg_out_ref, ss_out_ref, pl.program_id(1) == 0)
    else:
        (o_ref,) = rest
    o_ref[...] = x_new


def ple_update(xg, ss, p, wg, wp, x, g_next, layer):
    rows, k = xg.shape
    emit = g_next is not None
    g_spec, n_specs, n_shapes = _norm_specs(lambda i, j: (i, j), lambda i, j: (i, 0), TN_PLE)
    outs = pl.pallas_call(
        functools.partial(_ple_kernel, emit=emit),
        grid=(rows // TM, D_MODEL // TN_PLE),
        in_specs=[pl.BlockSpec((TM, k), lambda i, j: (i, 0)),
                  pl.BlockSpec((TM, LANES), lambda i, j: (i, 0)),
                  pl.BlockSpec((None, TM, D_PLE), lambda i, j: (layer, i, 0)),
                  pl.BlockSpec((None, k, TN_PLE), lambda i, j: (layer, 0, j)),
                  pl.BlockSpec((None, D_PLE, TN_PLE), lambda i, j: (layer, 0, j)),
                  pl.BlockSpec((TM, TN_PLE), lambda i, j: (i, j))] + ([g_spec] if emit else []),
        out_specs=[pl.BlockSpec((TM, TN_PLE), lambda i, j: (i, j))] + (n_specs if emit else []),
        out_shape=[jax.ShapeDtypeStruct((rows, D_MODEL), F32)] + (n_shapes if emit else []),
        compiler_params=_params("parallel", "arbitrary"),
        name="ple_update",
    )(xg, ss, p, wg, wp, x, *([g_next.reshape(1, -1)] if emit else []))
    return outs if emit else (outs[0], None, None)


def _swap_rope_halves(x):
    lane = lax.broadcasted_iota(jnp.int32, x.shape, 1)
    first = (lane % MLA_ROPE) < (MLA_ROPE // 2)
    return jnp.where(first, pltpu.roll(x, LANES - MLA_ROPE // 2, 1), pltpu.roll(x, MLA_ROPE // 2, 1))


def _mla_q_kernel(z_ref, g_ref, wt_ref, cos_ref, sin_ref, qt_ref):
    cq = _rms(z_ref[...], g_ref[...]).astype(BF16)
    qt = _dot_nt(wt_ref[...], cq)
    cos, sin = cos_ref[...], sin_ref[...]
    half = MLA_ROPE // 2
    for h in range(MLA_HEADS):
        base = h * MLA_QK
        x1 = qt[base + MLA_NOPE:base + MLA_NOPE + half]
        x2 = qt[base + MLA_NOPE + half:base + MLA_QK]
        qt_ref[h, :MLA_NOPE, :] = (qt[base:base + MLA_NOPE] * Q_SCALE).astype(BF16)
        qt_ref[h, MLA_NOPE:MLA_NOPE + half, :] = ((x1 * cos - x2 * sin) * Q_SCALE).astype(BF16)
        qt_ref[h, MLA_NOPE + half:, :] = ((x2 * cos + x1 * sin) * Q_SCALE).astype(BF16)


def mla_q(z, g, w_uq_t, cos_t, sin_t):
    rows = z.shape[0]
    tab_spec = pl.BlockSpec((MLA_ROPE // 2, TQ), lambda i: (0, i))
    return pl.pallas_call(
        _mla_q_kernel,
        grid=(rows // TQ,),
        in_specs=[pl.BlockSpec((TQ, MLA_Q_RANK), lambda i: (i, Z_CQ // MLA_Q_RANK)),
                  pl.BlockSpec((1, MLA_Q_RANK), lambda i: (0, 0)),
                  pl.BlockSpec(w_uq_t.shape, lambda i: (0, 0)),
                  tab_spec, tab_spec],
        out_specs=pl.BlockSpec((MLA_HEADS, MLA_QK, TQ), lambda i: (0, 0, i)),
        out_shape=jax.ShapeDtypeStruct((MLA_HEADS, MLA_QK, rows), BF16),
        compiler_params=_params("parallel"),
        name="mla_q",
    )(z, g.reshape(1, -1), w_uq_t, cos_t, sin_t)


def _mla_latent_kernel(zc_ref, zk_ref, g_ref, ck_ref, sk_ref, ckv_ref, kr_ref):
    ckv_ref[...] = _rms(zc_ref[...], g_ref[...])
    kr = zk_ref[...]
    rot = kr * ck_ref[...] + _swap_rope_halves(kr) * sk_ref[...]
    kr_ref[...] = rot[:, :MLA_ROPE]


def mla_latent(z, g, ck, sk):
    rows = z.shape[0]
    return pl.pallas_call(
        _mla_latent_kernel,
        grid=(rows // TR,),
        in_specs=[pl.BlockSpec((TR, MLA_KV_RANK), lambda i: (i, Z_CKV // MLA_KV_RANK)),
                  pl.BlockSpec((TR, LANES), lambda i: (i, Z_KR // LANES)),
                  pl.BlockSpec((1, MLA_KV_RANK), lambda i: (0, 0)),
                  pl.BlockSpec((TR, LANES), lambda i: (i, 0)),
                  pl.BlockSpec((TR, LANES), lambda i: (i, 0))],
        out_specs=[pl.BlockSpec((TR, MLA_KV_RANK), lambda i: (i, 0)),
                   pl.BlockSpec((TR, MLA_ROPE), lambda i: (i, 0))],
        out_shape=[jax.ShapeDtypeStruct((rows, MLA_KV_RANK), F32),
                   jax.ShapeDtypeStruct((rows, MLA_ROPE), F32)],
        compiler_params=_params("parallel"),
        name="mla_latent",
    )(z, z, g.reshape(1, -1), ck, sk)


def _kv_proj_kernel(ckv_ref, kr_ref, wk_ref, wvt_ref, k_ref, vt_ref):
    c = ckv_ref[...].astype(BF16)
    kn = _dot(c, wk_ref[...])
    vt = _dot_nt(wvt_ref[...], c)
    kr = kr_ref[...].astype(BF16)
    for h in range(MLA_HEADS):
        k_ref[h, :, :MLA_NOPE] = kn[:, h * MLA_NOPE:(h + 1) * MLA_NOPE].astype(BF16)
        k_ref[h, :, MLA_NOPE:] = kr
        vt_ref[h, 0] = vt[h * MLA_V:(h + 1) * MLA_V].astype(BF16)


def kv_proj(ckv, kr, w_uk, w_uv_t, tile, rows):
    return pl.pallas_call(
        _kv_proj_kernel,
        grid=(rows // tile,),
        in_specs=[pl.BlockSpec((tile, MLA_KV_RANK), lambda i: (i, 0)),
                  pl.BlockSpec((tile, MLA_ROPE), lambda i: (i, 0)),
                  pl.BlockSpec(w_uk.shape, lambda i: (0, 0)),
                  pl.BlockSpec(w_uv_t.shape, lambda i: (0, 0))],
        out_specs=[pl.BlockSpec((MLA_HEADS, tile, MLA_QK), lambda i: (0, i, 0)),
                   pl.BlockSpec((MLA_HEADS, 1, MLA_V, tile), lambda i: (0, i, 0, 0))],
        out_shape=[jax.ShapeDtypeStruct((MLA_HEADS, rows, MLA_QK), BF16),
                   jax.ShapeDtypeStruct((MLA_HEADS, rows // tile, MLA_V, tile), BF16)],
        compiler_params=_params("parallel"),
        name="kv_proj",
    )(ckv, kr, w_uk, w_uv_t)


def _attn_prompt_kernel(qt_ref, k_ref, vt_ref, o_ref, sa_sc, sb_sc, m_sc, l_sc, acc_sc):
    i = pl.program_id(1)
    m_sc[...] = jnp.full(m_sc.shape, -jnp.inf, F32)
    l_sc[...] = jnp.zeros(l_sc.shape, F32)
    acc_sc[...] = jnp.zeros(acc_sc.shape, F32)

    def scores(j, s_sc):
        start = pl.multiple_of(j * ATT_T, ATT_T)
        for hb in range(ATT_HB):
            s_sc[hb] = _dot(k_ref[hb, pl.ds(start, ATT_T), :], qt_ref[hb])

    def step(j, s_sc, masked):
        for hb in range(ATT_HB):
            s = s_sc[hb]
            if masked:
                kc = lax.broadcasted_iota(jnp.int32, s.shape, 0) // CHUNK
                qc = lax.broadcasted_iota(jnp.int32, s.shape, 1) // CHUNK
                s = jnp.where(kc <= qc, s, -jnp.inf)
            m_prev = m_sc[hb]
            m_new = jnp.maximum(m_prev, jnp.max(s, axis=0, keepdims=True))
            alpha = jnp.exp2(m_prev - m_new)
            p = jnp.exp2(s - m_new)
            l_sc[hb] = alpha * l_sc[hb] + jnp.sum(p, axis=0, keepdims=True)
            acc_sc[hb] = alpha * acc_sc[hb] + _dot(vt_ref[hb, j], p.astype(BF16))
            m_sc[hb] = m_new

    def pair(jj, carry):
        j = 2 * jj
        scores(j + 1, sb_sc)
        step(j, sa_sc, False)
        scores(j + 2, sa_sc)
        step(j + 1, sb_sc, False)
        return carry

    scores(0, sa_sc)
    lax.fori_loop(0, i // 2, pair, 0)

    @pl.when(i % 2 == 0)
    def _():
        step(i, sa_sc, True)

    @pl.when(i % 2 == 1)
    def _():
        scores(i, sb_sc)
        step(i - 1, sa_sc, False)
        step(i, sb_sc, True)

    for hb in range(ATT_HB):
        out_t = acc_sc[hb] / l_sc[hb]
        o_ref[:, hb * MLA_V:(hb + 1) * MLA_V] = out_t.T.astype(o_ref.dtype)


def attn_prompt(qt, k, vt):
    nblk = SEQ // ATT_T
    return pl.pallas_call(
        _attn_prompt_kernel,
        grid=(MLA_HEADS // ATT_HB, nblk),
        in_specs=[pl.BlockSpec((ATT_HB, MLA_QK, ATT_T), lambda g, i: (g, 0, i)),
                  pl.BlockSpec((ATT_HB, SEQ, MLA_QK), lambda g, i: (g, 0, 0)),
                  pl.BlockSpec((ATT_HB, nblk, MLA_V, ATT_T), lambda g, i: (g, 0, 0, 0))],
        out_specs=pl.BlockSpec((ATT_T, ATT_HB * MLA_V), lambda g, i: (i, g)),
        out_shape=jax.ShapeDtypeStruct((ROWS, MLA_HEADS * MLA_V), BF16),
        scratch_shapes=[pltpu.VMEM((ATT_HB, ATT_T, ATT_T), F32), pltpu.VMEM((ATT_HB, ATT_T, ATT_T), F32),
                        pltpu.VMEM((ATT_HB, 1, ATT_T), F32), pltpu.VMEM((ATT_HB, 1, ATT_T), F32),
                        pltpu.VMEM((ATT_HB, MLA_V, ATT_T), F32)],
        compiler_params=_params("parallel", "arbitrary"),
        name="attn_prompt",
    )(qt, k, vt)


def _attn_sample_kernel(qt_ref, ckv_ref, kr_ref, wk_ref, wv_ref, into_ref, o_ref, qlat_sc, qrope_sc, olat_sc):
    del into_ref
    b = pl.program_id(0)
    pad_rope = jnp.zeros((LANES - MLA_ROPE, N_SAMPLE), F32)

    @pl.when(b == 0)
    def _():
        for h in range(MLA_HEADS):
            q_nope = qt_ref[h, :MLA_NOPE, :].astype(F32).T.astype(BF16)
            qlat_sc[h] = _dot_nt(q_nope, wk_ref[:, h * MLA_NOPE:(h + 1) * MLA_NOPE])
            q_rope = jnp.concatenate([qt_ref[h, MLA_NOPE:, :].astype(F32), pad_rope], axis=0)
            qrope_sc[h] = q_rope.T

    rows = pl.ds(pl.multiple_of(b * DEC_SEQ, DEC_SEQ), DEC_SEQ)
    q_lat = jnp.concatenate([qlat_sc[h, rows, :] for h in range(MLA_HEADS)], axis=0)
    q_rope = jnp.concatenate([qrope_sc[h, rows, :] for h in range(MLA_HEADS)], axis=0)[:, :MLA_ROPE]
    c = ckv_ref[0].astype(BF16)
    s = _dot_nt(q_lat.astype(BF16), c) + _dot_nt(q_rope.astype(BF16), kr_ref[0].astype(BF16))
    kpos = lax.broadcasted_iota(jnp.int32, s.shape, 1)
    qpos = PAST_LEN + lax.broadcasted_iota(jnp.int32, s.shape, 0) % DEC_SEQ
    kchunk = jnp.where(kpos < SAMPLE_KEYS, kpos // CHUNK, SAMPLE_KEYS_PAD)
    s = jnp.where(kchunk <= qpos // CHUNK, s, -jnp.inf)
    p = jnp.exp2(s - jnp.max(s, axis=-1, keepdims=True))
    o_lat = _dot(p.astype(BF16), c) / jnp.sum(p, axis=-1, keepdims=True)
    for h in range(MLA_HEADS):
        olat_sc[h, rows, :] = o_lat[h * DEC_SEQ:(h + 1) * DEC_SEQ]

    @pl.when(b == DEC_BATCH - 1)
    def _():
        for h in range(MLA_HEADS):
            o_h = _dot(olat_sc[h].astype(BF16), wv_ref[:, h * MLA_V:(h + 1) * MLA_V])
            o_ref[:, h * MLA_V:(h + 1) * MLA_V] = o_h.astype(o_ref.dtype)


def attn_sample(qt, ckv, kr, w_uk, w_uv, into):
    return pl.pallas_call(
        _attn_sample_kernel,
        grid=(DEC_BATCH,),
        in_specs=[pl.BlockSpec((MLA_HEADS, MLA_QK, N_SAMPLE), lambda b: (0, 0, SEQ // N_SAMPLE)),
                  pl.BlockSpec((1, SAMPLE_KEYS_PAD, MLA_KV_RANK), lambda b: (b, 0, 0)),
                  pl.BlockSpec((1, SAMPLE_KEYS_PAD, MLA_ROPE), lambda b: (b, 0, 0)),
                  pl.BlockSpec(w_uk.shape, lambda b: (0, 0)),
                  pl.BlockSpec(w_uv.shape, lambda b: (0, 0)),
                  pl.BlockSpec(memory_space=pl.ANY)],
        out_specs=pl.BlockSpec((N_SAMPLE, MLA_HEADS * MLA_V), lambda b: (SEQ // N_SAMPLE, 0)),
        out_shape=jax.ShapeDtypeStruct(into.shape, into.dtype),
        input_output_aliases={5: 0},
        scratch_shapes=[pltpu.VMEM((MLA_HEADS, N_SAMPLE, MLA_KV_RANK), F32),
                        pltpu.VMEM((MLA_HEADS, N_SAMPLE, LANES), F32),
                        pltpu.VMEM((MLA_HEADS, N_SAMPLE, MLA_KV_RANK), F32)],
        compiler_params=_params("arbitrary"),
        name="attn_sample",
    )(qt, ckv, kr, w_uk, w_uv, into)


CONV_PAD = 32
CONV_OFF = CONV_PAD - CONV_HIST


def _conv_kernel(za_ref, zb_ref, past_ref, w_ref, b_ref, g_ref, beta_ref, *rest, t_rows):
    o_ref, hist_ref, ext_sc = rest[-3:]

    @pl.when(pl.program_id(1) == 0)
    def _():
        ext_sc[0:SUBLANES, :] = jnp.zeros((SUBLANES, CONV_CH), F32)
        ext_sc[CONV_OFF:CONV_PAD, :] = past_ref[0]

    u = za_ref[...] * _sigmoid(zb_ref[...])
    ext_sc[CONV_PAD:CONV_PAD + t_rows, :] = u
    cols = []
    for c0 in range(0, CONV_CH, LANES):
        acc = None
        for shift in range(SUBLANES):
            win = t_rows + (SUBLANES if shift else 0)
            group = None
            for base in range(0, CONV_PAD + SUBLANES, SUBLANES):
                tap = base + shift - CONV_OFF
                if 0 <= tap < CONV_WIDTH:
                    term = w_ref[tap:tap + 1, c0:c0 + LANES] * ext_sc[base:base + win, c0:c0 + LANES]
                    group = term if group is None else group + term
            part = group[shift:shift + t_rows]
            acc = part if acc is None else acc + part
        cols.append(acc)
    y = jnp.concatenate(cols, axis=1) + b_ref[...]
    yc = y - jnp.mean(y, axis=-1, keepdims=True)
    y = yc * lax.rsqrt(jnp.mean(yc * yc, axis=-1, keepdims=True) + EPS) * g_ref[...] + beta_ref[...]
    o_ref[...] = (y * _sigmoid(y)).astype(o_ref.dtype)
    tail = ext_sc[t_rows + CONV_OFF:t_rows + CONV_PAD, :]
    hist_ref[0] = tail
    ext_sc[CONV_OFF:CONV_PAD, :] = tail


def _into_args(into):
    if into is None:
        return [], [], None
    return [pl.BlockSpec(memory_space=pl.ANY)], [into], into


def conv_mixer(z, past, w, b, g, beta, *, batch, seq, row_off, t_rows, into=None):
    steps = seq // t_rows
    first = row_off // t_rows

    def rows(bi, t):
        return first + bi * steps + t

    vec = pl.BlockSpec((1, CONV_CH), lambda bi, t: (0, 0))
    extra_specs, extra_args, alias = _into_args(into)
    return pl.pallas_call(
        functools.partial(_conv_kernel, t_rows=t_rows),
        grid=(batch, steps),
        in_specs=[pl.BlockSpec((t_rows, CONV_CH), lambda bi, t: (rows(bi, t), Z_CA // CONV_CH)),
                  pl.BlockSpec((t_rows, CONV_CH), lambda bi, t: (rows(bi, t), Z_CB // CONV_CH)),
                  pl.BlockSpec((1, CONV_HIST, CONV_CH), lambda bi, t: (bi, 0, 0)),
                  pl.BlockSpec((CONV_WIDTH, CONV_CH), lambda bi, t: (0, 0)),
                  vec, vec, vec] + extra_specs,
        out_specs=[pl.BlockSpec((t_rows, CONV_CH), lambda bi, t: (rows(bi, t), 0)),
                   pl.BlockSpec((1, CONV_HIST, CONV_CH), lambda bi, t: (bi, 0, 0))],
        out_shape=[jax.ShapeDtypeStruct((ROWS, CONV_CH), BF16),
                   jax.ShapeDtypeStruct((batch, CONV_HIST, CONV_CH), F32)],
        input_output_aliases={} if alias is None else {7: 0},
        scratch_shapes=[pltpu.VMEM((CONV_PAD + t_rows, CONV_CH), F32)],
        compiler_params=_params("parallel", "arbitrary"),
        name="conv_mixer",
    )(z, z, past, w, b.reshape(1, -1), g.reshape(1, -1), beta.reshape(1, -1), *extra_args)


def _pad_rows(x, rows):
    if x.shape[0] == rows:
        return x
    return jnp.concatenate([x, jnp.zeros((rows - x.shape[0], x.shape[1]), x.dtype)], axis=0)


def _log_sigmoid(x):
    return jnp.minimum(x, 0.0) - jnp.log1p(jnp.exp(-jnp.abs(x)))


def _gla_kernel(q_ref, k_ref, v_ref, r_ref, al_ref, wa_ref, ba_ref, gn_ref, s0_ref, *rest, c, cps):
    o_ref, s_out_ref, s_sc = rest[-3:]
    step = pl.program_id(1)

    @pl.when(step == 0)
    def _():
        s_sc[...] = s0_ref[0]

    row = lax.broadcasted_iota(jnp.int32, (GLA_CP, GLA_CP), 0)
    col = lax.broadcasted_iota(jnp.int32, (GLA_CP, GLA_CP), 1)
    causal = col <= row
    rows_p = cps * GLA_CP
    gate = _dot(al_ref[...].astype(BF16), wa_ref[...]) + ba_ref[...]
    log_a = _pad_rows(_log_sigmoid(gate) / GLA_TAU, rows_p)
    row_p = lax.broadcasted_iota(jnp.int32, (rows_p, rows_p), 0)
    col_p = lax.broadcasted_iota(jnp.int32, (rows_p, rows_p), 1)
    tri = jnp.where(col_p <= row_p, jnp.where(row_p // GLA_CP == col_p // GLA_CP, 1.0, 0.0), 0.0)
    bcum_all = jnp.dot(tri, log_a, precision=lax.Precision.HIGHEST, preferred_element_type=F32)
    grow_all = jnp.exp(bcum_all)
    shrink_all = jnp.exp(-bcum_all)
    for ci in range(cps):
        r0 = ci * c
        p0 = ci * GLA_CP
        for h in range(GLA_HEADS):
            ks = slice(h * GLA_DK, (h + 1) * GLA_DK)
            vs = slice(h * GLA_DV, (h + 1) * GLA_DV)
            b = bcum_all[p0:p0 + GLA_CP, ks]
            q = _pad_rows(q_ref[r0:r0 + c, ks], GLA_CP) * GLA_DK ** -0.5
            k = _pad_rows(k_ref[r0:r0 + c, ks], GLA_CP)
            v = _pad_rows(v_ref[r0:r0 + c, vs], GLA_CP).astype(BF16)
            q_t = (q * grow_all[p0:p0 + GLA_CP, ks]).astype(BF16)
            k_t = (k * shrink_all[p0:p0 + GLA_CP, ks]).astype(BF16)
            a = jnp.where(causal, _dot_nt(q_t, k_t), 0.0)
            s_prev = s_sc[h]
            o = _dot(q_t, s_prev.astype(BF16)) + _dot(a.astype(BF16), v)
            b_last = b[c - 1:c, :]
            k_dec = k * jnp.exp(b_last - b)
            k_dec_t = _pad_rows(k_dec, XPOSE).T
            decay = jnp.exp(_pad_rows(b, XPOSE).T[:, c - 1:c])
            s_sc[h] = decay * s_prev + _dot(k_dec_t.astype(BF16), _pad_rows(v, XPOSE))
            on = _rms(o[:c], gn_ref[...])
            r = r_ref[r0:r0 + c, vs]
            o_ref[r0:r0 + c, vs] = (on * (r * _sigmoid(r))).astype(o_ref.dtype)

    @pl.when(step == pl.num_programs(1) - 1)
    def _():
        s_out_ref[0] = s_sc[...]


def gla_mixer(z, s0, wa2, ba, gn, *, batch, seq, row_off, c, cps, into=None):
    t_rows = c * cps
    steps = seq // t_rows
    first = row_off // t_rows

    def rows(bi, t):
        return first + bi * steps + t

    state = pl.BlockSpec((1, GLA_HEADS, GLA_DK, GLA_DV), lambda bi, t: (bi, 0, 0, 0))
    extra_specs, extra_args, alias = _into_args(into)
    return pl.pallas_call(
        functools.partial(_gla_kernel, c=c, cps=cps),
        grid=(batch, steps),
        in_specs=[pl.BlockSpec((t_rows, HK), lambda bi, t: (rows(bi, t), Z_GQ // HK)),
                  pl.BlockSpec((t_rows, HK), lambda bi, t: (rows(bi, t), Z_GK // HK)),
                  pl.BlockSpec((t_rows, HV), lambda bi, t: (rows(bi, t), Z_GV // HV)),
                  pl.BlockSpec((t_rows, HV), lambda bi, t: (rows(bi, t), Z_GR // HV)),
                  pl.BlockSpec((t_rows, LANES), lambda bi, t: (rows(bi, t), Z_AL // LANES)),
                  pl.BlockSpec((LANES, HK), lambda bi, t: (0, 0)),
                  pl.BlockSpec((1, HK), lambda bi, t: (0, 0)),
                  pl.BlockSpec((1, GLA_DV), lambda bi, t: (0, 0)),
                  state] + extra_specs,
        out_specs=[pl.BlockSpec((t_rows, HV), lambda bi, t: (rows(bi, t), 0)), state],
        out_shape=[jax.ShapeDtypeStruct((ROWS, HV), BF16),
                   jax.ShapeDtypeStruct((batch, GLA_HEADS, GLA_DK, GLA_DV), F32)],
        input_output_aliases={} if alias is None else {9: 0},
        scratch_shapes=[pltpu.VMEM((GLA_HEADS, GLA_DK, GLA_DV), F32)],
        compiler_params=_params("parallel", "arbitrary"),
        name="gla_mixer",
    )(z, z, z, z, z, wa2, ba.reshape(1, -1), gn.reshape(1, -1), s0, *extra_args)


_SRC_CONV = MLA_Q_RANK + MLA_KV_RANK + MLA_ROPE
_SRC_GLA = _SRC_CONV + 2 * CONV_CH
D_IN = _SRC_GLA + 2 * HK + 2 * HV + GLA_GATE_RANK
_W_IN_PIECES = (
    (_SRC_CONV, CONV_CH, Z_CA), (_SRC_CONV + CONV_CH, CONV_CH, Z_CB),
    (_SRC_GLA + 2 * HK, HV, Z_GV), (_SRC_GLA + 2 * HK + HV, HV, Z_GR),
    (0, MLA_Q_RANK, Z_CQ),
    (_SRC_GLA, HK, Z_GQ), (_SRC_GLA + HK, HK, Z_GK),
    (MLA_Q_RANK, MLA_KV_RANK, Z_CKV),
    (MLA_Q_RANK + MLA_KV_RANK, MLA_ROPE, Z_KR),
    (_SRC_GLA + 2 * HK + 2 * HV, GLA_GATE_RANK, Z_AL),
)
PACK_SRC = 64


def _pack_tables():
    first, valid = [], []
    for col in range(0, Z_W, LANES):
        src, width, dst = next(p for p in _W_IN_PIECES if p[2] <= col < p[2] + max(p[1], LANES))
        first.append((src + col - dst) // PACK_SRC)
        valid.append(min(LANES, width))
    return jnp.asarray(first, jnp.int32), jnp.asarray(valid, jnp.int32)


def _pack_w_in_kernel(first_ref, valid_ref, lo_ref, hi_ref, o_ref):
    del first_ref
    x = jnp.concatenate([lo_ref[...], hi_ref[...]], axis=0)
    row = lax.broadcasted_iota(jnp.int32, x.shape, 0)
    x = jnp.where(row < valid_ref[pl.program_id(1)], x, 0.0)
    o_ref[...] = x.T.astype(BF16)


def pack_w_in(w_in_t):
    first, valid = _pack_tables()
    last = (D_IN - 1) // PACK_SRC
    return pl.pallas_call(
        _pack_w_in_kernel,
        grid_spec=pltpu.PrefetchScalarGridSpec(
            num_scalar_prefetch=2,
            grid=(DEPTH, Z_W // LANES),
            in_specs=[pl.BlockSpec((None, PACK_SRC, D_MODEL), lambda l, c, first, valid: (l, first[c], 0)),
                      pl.BlockSpec((None, PACK_SRC, D_MODEL),
                                   lambda l, c, first, valid: (l, jnp.minimum(first[c] + 1, last), 0))],
            out_specs=pl.BlockSpec((None, D_MODEL, LANES), lambda l, c, first, valid: (l, 0, c)),
        ),
        out_shape=jax.ShapeDtypeStruct((DEPTH, D_MODEL, Z_W), BF16),
        compiler_params=_params("parallel", "arbitrary"),
        name="pack_w_in",
    )(first, valid, w_in_t, w_in_t)


def _rope_tables():
    half = MLA_ROPE // 2
    pos = jnp.concatenate([jnp.arange(SEQ, dtype=jnp.int32),
                           jnp.tile(PAST_LEN + jnp.arange(DEC_SEQ, dtype=jnp.int32), DEC_BATCH)])
    inv_freq = ROPE_THETA ** (-jnp.arange(half, dtype=F32) / half)
    ang = pos.astype(F32)[:, None] * inv_freq[None, :]
    cos, sin = jnp.cos(ang), jnp.sin(ang)
    c64 = jnp.concatenate([cos, cos], axis=1)
    s64 = jnp.concatenate([-sin, sin], axis=1)
    zero = jnp.zeros_like(s64)
    key = (jnp.concatenate([c64, zero], axis=1), jnp.concatenate([s64, zero], axis=1))
    return (cos.T, sin.T), key


@jax.jit
def _forward(x_prompt, x_sample, cache_ckv, cache_krope, cache_conv, state_gla, p_prompt, p_sample, norm_mix,
             w_in, mla_q_norm, mla_kv_norm, mla_w_uq, mla_w_uk, mla_w_uv, conv_dw_w, conv_dw_b, conv_ln_g,
             conv_ln_b, gla_w_a2, gla_b_a, gla_norm, w_out, norm_ffn, ffn_w_gate, ffn_w_up, ffn_w_down,
             norm_ple, ple_w_gate, ple_w_proj, norm_final):
    w_in_p = pack_w_in(jnp.swapaxes(w_in, 1, 2))
    w_uq_t = mla_w_uq.astype(BF16).transpose(0, 2, 1)
    w_uk = mla_w_uk.reshape(DEPTH, MLA_KV_RANK, MLA_HEADS * MLA_NOPE).astype(BF16)
    w_uv = mla_w_uv.reshape(DEPTH, MLA_KV_RANK, MLA_HEADS * MLA_V).astype(BF16)
    w_uv_t = w_uv.transpose(0, 2, 1)
    w_a2 = jnp.concatenate([gla_w_a2.astype(BF16),
                            jnp.zeros((DEPTH, LANES - GLA_GATE_RANK, HK), BF16)], axis=1)
    w_out_b = w_out.astype(BF16)
    w_gate, w_up = ffn_w_gate, ffn_w_up
    w_down = ffn_w_down.astype(BF16)
    w_pg, w_pp = ple_w_gate.astype(BF16), ple_w_proj.astype(BF16)
    (cos_t, sin_t), (ck, sk) = _rope_tables()

    x = jnp.concatenate([x_prompt.reshape(SEQ, D_MODEL), x_sample.reshape(N_SAMPLE, D_MODEL)], axis=0)
    p_all = jnp.concatenate([p_prompt.reshape(DEPTH, SEQ, D_PLE), p_sample.reshape(DEPTH, N_SAMPLE, D_PLE)], axis=1)
    zero_conv = jnp.zeros((1, CONV_HIST, CONV_CH), F32)
    zero_gla = jnp.zeros((1, GLA_HEADS, GLA_DK, GLA_DV), F32)

    ckvs, krs, convs_p, convs_s, glas_p, glas_s = [], [], [], [], [], []
    xg, ss = norm_prep(x, norm_mix[0])
    for i in range(DEPTH):
        z = in_proj(xg, ss, w_in_p, i)

        qt = mla_q(z, mla_q_norm[i], w_uq_t[i], cos_t, sin_t)
        ckv, kr = mla_latent(z, mla_kv_norm[i], ck, sk)
        k_p, vt_p = kv_proj(ckv, kr, w_uk[i], w_uv_t[i], ATT_T, SEQ)
        pad = SAMPLE_KEYS_PAD - SAMPLE_KEYS
        ckv_s = jnp.concatenate([cache_ckv[i], ckv[SEQ:].reshape(DEC_BATCH, DEC_SEQ, MLA_KV_RANK),
                                 jnp.zeros((DEC_BATCH, pad, MLA_KV_RANK), F32)], axis=1)
        kr_s = jnp.concatenate([cache_krope[i], kr[SEQ:].reshape(DEC_BATCH, DEC_SEQ, MLA_ROPE),
                                jnp.zeros((DEC_BATCH, pad, MLA_ROPE), F32)], axis=1)
        o_a = attn_sample(qt, ckv_s, kr_s, w_uk[i], w_uv[i], into=attn_prompt(qt, k_p, vt_p))

        conv_args = (conv_dw_w[i], conv_dw_b[i], conv_ln_g[i], conv_ln_b[i])
        o_b, conv_p = conv_mixer(z, zero_conv, *conv_args, batch=1, seq=SEQ, row_off=0, t_rows=CONV_T)
        o_b, conv_s = conv_mixer(z, cache_conv[i], *conv_args, batch=DEC_BATCH, seq=DEC_SEQ, row_off=SEQ,
                                 t_rows=DEC_SEQ, into=o_b)

        gla_args = (w_a2[i], gla_b_a[i], gla_norm[i])
        o_c, gla_p = gla_mixer(z, zero_gla, *gla_args, batch=1, seq=SEQ, row_off=0, c=CHUNK, cps=GLA_CPS)
        o_c, gla_s = gla_mixer(z, state_gla[i], *gla_args, batch=DEC_BATCH, seq=DEC_SEQ, row_off=SEQ,
                               c=DEC_SEQ, cps=1, into=o_c)

        x, xg, ss = out_proj(o_a, o_b, o_c, w_out_b, x, norm_ffn[i], i)
        x, xg, ss = ffn_down(ffn_gate_up(xg, ss, w_gate, w_up, i), w_down, x, norm_ple[i], i)
        g_next = norm_mix[i + 1] if i + 1 < DEPTH else None
        x, xg, ss = ple_update(xg, ss, p_all, w_pg, w_pp, x, g_next, i)

        ckvs.append(ckv)
        krs.append(kr)
        convs_p.append(conv_p)
        convs_s.append(conv_s)
        glas_p.append(gla_p)
        glas_s.append(gla_s)

    y_prompt = rmsnorm_rows(x, norm_final, 0, SEQ, TY)
    y_sample = rmsnorm_rows(x, norm_final, SEQ, N_SAMPLE, N_SAMPLE)
    ckv_all, kr_all = jnp.stack(ckvs), jnp.stack(krs)
    return (y_prompt.reshape(1, SEQ, D_MODEL),
            y_sample.reshape(DEC_BATCH, DEC_SEQ, D_MODEL),
            ckv_all[:, :SEQ].reshape(DEPTH, 1, SEQ, MLA_KV_RANK),
            kr_all[:, :SEQ].reshape(DEPTH, 1, SEQ, MLA_ROPE),
            jnp.stack(convs_p),
            jnp.stack(glas_p),
            ckv_all[:, SEQ:].reshape(DEPTH, DEC_BATCH, DEC_SEQ, MLA_KV_RANK),
            kr_all[:, SEQ:].reshape(DEPTH, DEC_BATCH, DEC_SEQ, MLA_ROPE),
            jnp.stack(convs_s),
            jnp.stack(glas_s))


def kernel(x_prompt, x_sample, cache_ckv, cache_krope, cache_conv, state_gla, p_prompt, p_sample, norm_mix, w_in, mla_q_norm, mla_kv_norm, mla_w_uq, mla_w_uk, mla_w_uv, conv_dw_w, conv_dw_b, conv_ln_g, conv_ln_b, gla_w_a2, gla_b_a, gla_norm, w_out, norm_ffn, ffn_w_gate, ffn_w_up, ffn_w_down, norm_ple, ple_w_gate, ple_w_proj, norm_final):
    return _forward(x_prompt, x_sample, cache_ckv, cache_krope, cache_conv, state_gla, p_prompt, p_sample,
                    norm_mix, w_in, mla_q_norm, mla_kv_norm, mla_w_uq, mla_w_uk, mla_w_uv, conv_dw_w, conv_dw_b,
                    conv_ln_g, conv_ln_b, gla_w_a2, gla_b_a, gla_norm, w_out, norm_ffn, ffn_w_gate, ffn_w_up,
                    ffn_w_down, norm_ple, ple_w_gate, ple_w_proj, norm_final)
```

```python
import functools

import jax
import jax.numpy as jnp
from jax import lax
from jax.experimental import pallas as pl
from jax.experimental.pallas import tpu as pltpu

F32 = jnp.float32
BF16 = jnp.bfloat16

D_MODEL = 4096
SEQ = 8192
DEPTH = 4
DEC_BATCH = 8
DEC_SEQ = 16
PAST_LEN = 1024
CHUNK = 64
EPS = 1e-6
D_PLE = 256
D_FF = 11008
MLA_HEADS = 16
MLA_Q_RANK = 1024
MLA_KV_RANK = 512
MLA_NOPE = 128
MLA_ROPE = 64
MLA_V = 128
MLA_QK = MLA_NOPE + MLA_ROPE
ROPE_THETA = 10000.0
CONV_CH = 1024
CONV_WIDTH = 31
CONV_HIST = CONV_WIDTH - 1
GLA_HEADS = 4
GLA_DK = 128
GLA_DV = 256
GLA_GATE_RANK = 16
GLA_TAU = 16.0
HK = GLA_HEADS * GLA_DK
HV = GLA_HEADS * GLA_DV

N_SAMPLE = DEC_BATCH * DEC_SEQ
ROWS = SEQ + N_SAMPLE
SAMPLE_KEYS = PAST_LEN + DEC_SEQ
SAMPLE_KEYS_PAD = 1152

LANES = 128
SUBLANES = 8
VMEM_LIMIT = 52 * 1024 * 1024

Z_CA = 0
Z_CB = 1024
Z_GV = 2048
Z_GR = 3072
Z_CQ = 4096
Z_GQ = 5120
Z_GK = 5632
Z_CKV = 6144
Z_KR = 6656
Z_AL = 6784
Z_W = 6912

TM = 1040
TM_FF = 1664
TR = 416
TY = 512
TN_IN = 768
TN_OUT = 512
TN_FF = 256
TM_DOWN = 416
TN_DOWN = 512
TN_PLE = 512
TQ = 640
ATT_T = 512
ATT_HB = 2
Q_SCALE = MLA_QK ** -0.5 * 1.4426950408889634
CONV_T = 256
GLA_CPS = 4
GLA_CP = 64
XPOSE = 128


def _params(*sem):
    return pltpu.CompilerParams(dimension_semantics=sem, vmem_limit_bytes=VMEM_LIMIT)


def _rms(x, g):
    return x * lax.rsqrt(jnp.mean(x * x, axis=-1, keepdims=True) + EPS) * g


def _sigmoid(x):
    return 1.0 / (1.0 + jnp.exp(-x))


def _dot(a, b):
    return jnp.dot(a, b, preferred_element_type=F32)


def _dot_nt(a, b):
    return lax.dot_general(a, b, (((1,), (1,)), ((), ())), preferred_element_type=F32)


def _fold_lanes(sq):
    part = sq[:, :LANES]
    for c in range(LANES, sq.shape[1], LANES):
        part = part + sq[:, c:c + LANES]
    return part


def _emit_xg(x_new, g_ref, xg_ref, cols):
    xg_ref[:, cols] = (x_new * g_ref[:, cols]).astype(BF16)
    return _fold_lanes(x_new * x_new)


def _accumulate_ss(ss_ref, part, first):
    @pl.when(first)
    def _():
        ss_ref[...] = part

    @pl.when(jnp.logical_not(first))
    def _():
        ss_ref[...] += part


def _emit_norm_inputs(x_new, g_ref, xg_ref, ss_ref, first):
    _accumulate_ss(ss_ref, _emit_xg(x_new, g_ref, xg_ref, slice(None)), first)


def _column_halves(ref):
    half = ref.shape[1] // 2
    return slice(0, half), slice(half, 2 * half)


def _row_scale(ss_ref):
    return lax.rsqrt(jnp.sum(ss_ref[...], axis=-1, keepdims=True) * (1.0 / D_MODEL) + EPS)


def _norm_specs(imap_tile, imap_row, tn, tm=None):
    tm = TM if tm is None else tm
    return (pl.BlockSpec((1, tn), lambda *a: (0, imap_tile(*a)[1])),
            [pl.BlockSpec((tm, tn), imap_tile), pl.BlockSpec((tm, LANES), imap_row)],
            [jax.ShapeDtypeStruct((ROWS, D_MODEL), BF16), jax.ShapeDtypeStruct((ROWS, LANES), F32)])


def _norm_prep_kernel(x_ref, g_ref, xg_ref, ss_ref):
    x = x_ref[...]
    xg_ref[...] = (x * g_ref[...]).astype(BF16)
    ss_ref[...] = _fold_lanes(x * x)


def norm_prep(x, g):
    rows, d = x.shape
    return pl.pallas_call(
        _norm_prep_kernel,
        grid=(rows // TR,),
        in_specs=[pl.BlockSpec((TR, d), lambda i: (i, 0)),
                  pl.BlockSpec((1, d), lambda i: (0, 0))],
        out_specs=[pl.BlockSpec((TR, d), lambda i: (i, 0)), pl.BlockSpec((TR, LANES), lambda i: (i, 0))],
        out_shape=[jax.ShapeDtypeStruct((rows, d), BF16), jax.ShapeDtypeStruct((rows, LANES), F32)],
        compiler_params=_params("parallel"),
        name="norm_prep",
    )(x, g.reshape(1, d))


def _rmsnorm_kernel(x_ref, g_ref, o_ref):
    o_ref[...] = _rms(x_ref[...], g_ref[...]).astype(o_ref.dtype)


def rmsnorm_rows(x, g, row_off, rows, tile):
    d = x.shape[1]
    first = row_off // tile
    return pl.pallas_call(
        _rmsnorm_kernel,
        grid=(rows // tile,),
        in_specs=[pl.BlockSpec((tile, d), lambda i: (first + i, 0)),
                  pl.BlockSpec((1, d), lambda i: (0, 0))],
        out_specs=pl.BlockSpec((tile, d), lambda i: (i, 0)),
        out_shape=jax.ShapeDtypeStruct((rows, d), F32),
        compiler_params=_params("parallel"),
        name="rmsnorm_rows",
    )(x, g.reshape(1, d))


def _in_proj_kernel(xg_ref, ss_ref, w_ref, o_ref):
    o_ref[...] = _dot(xg_ref[...], w_ref[...]) * _row_scale(ss_ref)


def in_proj(xg, ss, w, layer):
    rows, k = xg.shape
    n = w.shape[2]
    return pl.pallas_call(
        _in_proj_kernel,
        grid=(rows // TM, n // TN_IN),
        in_specs=[pl.BlockSpec((TM, k), lambda i, j: (i, 0)),
                  pl.BlockSpec((TM, LANES), lambda i, j: (i, 0)),
                  pl.BlockSpec((None, k, TN_IN), lambda i, j: (layer, 0, j))],
        out_specs=pl.BlockSpec((TM, TN_IN), lambda i, j: (i, j)),
        out_shape=jax.ShapeDtypeStruct((rows, n), F32),
        compiler_params=_params("parallel", "arbitrary"),
        name="in_proj",
    )(xg, ss, w)


def _out_proj_kernel(oa_ref, ob_ref, oc_ref, wa_ref, wb_ref, wc_ref, x_ref, g_ref, o_ref, xg_ref, ss_ref):
    acc = _dot(oa_ref[...], wa_ref[...])
    acc += _dot(ob_ref[...], wb_ref[...])
    acc += _dot(oc_ref[...], wc_ref[...])
    x_new = x_ref[...] + acc
    o_ref[...] = x_new
    _emit_norm_inputs(x_new, g_ref, xg_ref, ss_ref, pl.program_id(1) == 0)


def out_proj(o_a, o_b, o_c, w, x, g_next, layer):
    rows = x.shape[0]
    da, db, dc = o_a.shape[1], o_b.shape[1], o_c.shape[1]
    g_spec, n_specs, n_shapes = _norm_specs(lambda i, j: (i, j), lambda i, j: (i, 0), TN_OUT)
    return pl.pallas_call(
        _out_proj_kernel,
        grid=(rows // TM, D_MODEL // TN_OUT),
        in_specs=[pl.BlockSpec((TM, da), lambda i, j: (i, 0)),
                  pl.BlockSpec((TM, db), lambda i, j: (i, 0)),
                  pl.BlockSpec((TM, dc), lambda i, j: (i, 0)),
                  pl.BlockSpec((None, da, TN_OUT), lambda i, j: (layer, 0, j)),
                  pl.BlockSpec((None, db, TN_OUT), lambda i, j: (layer, da // db, j)),
                  pl.BlockSpec((None, dc, TN_OUT), lambda i, j: (layer, (da + db) // dc, j)),
                  pl.BlockSpec((TM, TN_OUT), lambda i, j: (i, j)),
                  g_spec],
        out_specs=[pl.BlockSpec((TM, TN_OUT), lambda i, j: (i, j))] + n_specs,
        out_shape=[jax.ShapeDtypeStruct((rows, D_MODEL), F32)] + n_shapes,
        compiler_params=_params("parallel", "arbitrary"),
        name="out_proj",
    )(o_a, o_b, o_c, w, w, w, x, g_next.reshape(1, -1))


def _ffn_gate_up_kernel(xg_ref, ss_ref, wg_ref, wu_ref, o_ref):
    xg = xg_ref[...]
    r = _row_scale(ss_ref)
    g = _dot(xg, wg_ref[...].astype(BF16)) * r
    u = _dot(xg, wu_ref[...].astype(BF16)) * r
    o_ref[...] = (g * _sigmoid(g) * u).astype(o_ref.dtype)


def ffn_gate_up(xg, ss, wg, wu, layer):
    rows, k = xg.shape
    return pl.pallas_call(
        _ffn_gate_up_kernel,
        grid=(rows // TM_FF, D_FF // TN_FF),
        in_specs=[pl.BlockSpec((TM_FF, k), lambda i, j: (i, 0)),
                  pl.BlockSpec((TM_FF, LANES), lambda i, j: (i, 0)),
                  pl.BlockSpec((None, k, TN_FF), lambda i, j: (layer, 0, j)),
                  pl.BlockSpec((None, k, TN_FF), lambda i, j: (layer, 0, j))],
        out_specs=pl.BlockSpec((TM_FF, TN_FF), lambda i, j: (i, j)),
        out_shape=jax.ShapeDtypeStruct((rows, D_FF), BF16),
        compiler_params=_params("parallel", "arbitrary"),
        name="ffn_gate_up",
    )(xg, ss, wg, wu)


def _ffn_down_kernel(a_ref, w_ref, x_ref, g_ref, o_ref, xg_ref, ss_ref):
    x_new = x_ref[...] + _dot(a_ref[...], w_ref[...])
    o_ref[...] = x_new
    _emit_norm_inputs(x_new, g_ref, xg_ref, ss_ref, pl.program_id(1) == 0)


def ffn_down(a, w, x, g_next, layer):
    rows, k = a.shape
    g_spec, n_specs, n_shapes = _norm_specs(lambda i, j: (i, j), lambda i, j: (i, 0), TN_DOWN, TM_DOWN)
    return pl.pallas_call(
        _ffn_down_kernel,
        grid=(rows // TM_DOWN, D_MODEL // TN_DOWN),
        in_specs=[pl.BlockSpec((TM_DOWN, k), lambda i, j: (i, 0)),
                  pl.BlockSpec((None, k, TN_DOWN), lambda i, j: (layer, 0, j)),
                  pl.BlockSpec((TM_DOWN, TN_DOWN), lambda i, j: (i, j)),
                  g_spec],
        out_specs=[pl.BlockSpec((TM_DOWN, TN_DOWN), lambda i, j: (i, j))] + n_specs,
        out_shape=[jax.ShapeDtypeStruct((rows, D_MODEL), F32)] + n_shapes,
        compiler_params=_params("parallel", "arbitrary"),
        name="ffn_down",
    )(a, w, x, g_next.reshape(1, -1))


def _ple_kernel(xg_ref, ss_ref, p_ref, wg_ref, wp_ref, x_ref, *rest, emit):
    r = _row_scale(ss_ref)
    p = p_ref[...].astype(BF16)
    if emit:
        g_ref, o_ref, xg_out_ref, ss_out_ref = rest
    else:
        (o_ref,) = rest
    part = 0.0
    for cols in _column_halves(o_ref):
        gate = _sigmoid(_dot(xg_ref[...], wg_ref[:, cols]) * r)
        x_new = x_ref[:, cols] + gate * _dot(p, wp_ref[:, cols])
        o_ref[:, cols] = x_new
        if emit:
            part = part + _emit_xg(x_new, g_ref, xg_out_ref, cols)
    if emit:
        _accumulate_ss(ss_out_ref, part, pl.program_id(1) == 0)


def ple_update(xg, ss, p, wg, wp, x, g_next, layer):
    rows, k = xg.shape
    emit = g_next is not None
    g_spec, n_specs, n_shapes = _norm_specs(lambda i, j: (i, j), lambda i, j: (i, 0), TN_PLE)
    outs = pl.pallas_call(
        functools.partial(_ple_kernel, emit=emit),
        grid=(rows // TM, D_MODEL // TN_PLE),
        in_specs=[pl.BlockSpec((TM, k), lambda i, j: (i, 0)),
                  pl.BlockSpec((TM, LANES), lambda i, j: (i, 0)),
                  pl.BlockSpec((None, TM, D_PLE), lambda i, j: (layer, i, 0)),
                  pl.BlockSpec((None, k, TN_PLE), lambda i, j: (layer, 0, j)),
                  pl.BlockSpec((None, D_PLE, TN_PLE), lambda i, j: (layer, 0, j)),
                  pl.BlockSpec((TM, TN_PLE), lambda i, j: (i, j))] + ([g_spec] if emit else []),
        out_specs=[pl.BlockSpec((TM, TN_PLE), lambda i, j: (i, j))] + (n_specs if emit else []),
        out_shape=[jax.ShapeDtypeStruct((rows, D_MODEL), F32)] + (n_shapes if emit else []),
        compiler_params=_params("parallel", "arbitrary"),
        name="ple_update",
    )(xg, ss, p, wg, wp, x, *([g_next.reshape(1, -1)] if emit else []))
    return outs if emit else (outs[0], None, None)


def _swap_rope_halves(x):
    lane = lax.broadcasted_iota(jnp.int32, x.shape, 1)
    first = (lane % MLA_ROPE) < (MLA_ROPE // 2)
    return jnp.where(first, pltpu.roll(x, LANES - MLA_ROPE // 2, 1), pltpu.roll(x, MLA_ROPE // 2, 1))


def _mla_q_kernel(z_ref, g_ref, wt_ref, cos_ref, sin_ref, qt_ref):
    cq = _rms(z_ref[...], g_ref[...]).astype(BF16)
    qt = _dot_nt(wt_ref[...], cq)
    cos, sin = cos_ref[...], sin_ref[...]
    half = MLA_ROPE // 2
    for h in range(MLA_HEADS):
        base = h * MLA_QK
        x1 = qt[base + MLA_NOPE:base + MLA_NOPE + half]
        x2 = qt[base + MLA_NOPE + half:base + MLA_QK]
        qt_ref[h, :MLA_NOPE, :] = (qt[base:base + MLA_NOPE] * Q_SCALE).astype(BF16)
        qt_ref[h, MLA_NOPE:MLA_NOPE + half, :] = ((x1 * cos - x2 * sin) * Q_SCALE).astype(BF16)
        qt_ref[h, MLA_NOPE + half:, :] = ((x2 * cos + x1 * sin) * Q_SCALE).astype(BF16)


def mla_q(z, g, w_uq_t, cos_t, sin_t):
    rows = z.shape[0]
    tab_spec = pl.BlockSpec((MLA_ROPE // 2, TQ), lambda i: (0, i))
    return pl.pallas_call(
        _mla_q_kernel,
        grid=(rows // TQ,),
        in_specs=[pl.BlockSpec((TQ, MLA_Q_RANK), lambda i: (i, Z_CQ // MLA_Q_RANK)),
                  pl.BlockSpec((1, MLA_Q_RANK), lambda i: (0, 0)),
                  pl.BlockSpec(w_uq_t.shape, lambda i: (0, 0)),
                  tab_spec, tab_spec],
        out_specs=pl.BlockSpec((MLA_HEADS, MLA_QK, TQ), lambda i: (0, 0, i)),
        out_shape=jax.ShapeDtypeStruct((MLA_HEADS, MLA_QK, rows), BF16),
        compiler_params=_params("parallel"),
        name="mla_q",
    )(z, g.reshape(1, -1), w_uq_t, cos_t, sin_t)


def _mla_latent_kernel(zc_ref, zk_ref, g_ref, ck_ref, sk_ref, ckv_ref, kr_ref):
    ckv_ref[...] = _rms(zc_ref[...], g_ref[...])
    kr = zk_ref[...]
    rot = kr * ck_ref[...] + _swap_rope_halves(kr) * sk_ref[...]
    kr_ref[...] = rot[:, :MLA_ROPE]


def mla_latent(z, g, ck, sk):
    rows = z.shape[0]
    return pl.pallas_call(
        _mla_latent_kernel,
        grid=(rows // TR,),
        in_specs=[pl.BlockSpec((TR, MLA_KV_RANK), lambda i: (i, Z_CKV // MLA_KV_RANK)),
                  pl.BlockSpec((TR, LANES), lambda i: (i, Z_KR // LANES)),
                  pl.BlockSpec((1, MLA_KV_RANK), lambda i: (0, 0)),
                  pl.BlockSpec((TR, LANES), lambda i: (i, 0)),
                  pl.BlockSpec((TR, LANES), lambda i: (i, 0))],
        out_specs=[pl.BlockSpec((TR, MLA_KV_RANK), lambda i: (i, 0)),
                   pl.BlockSpec((TR, MLA_ROPE), lambda i: (i, 0))],
        out_shape=[jax.ShapeDtypeStruct((rows, MLA_KV_RANK), F32),
                   jax.ShapeDtypeStruct((rows, MLA_ROPE), F32)],
        compiler_params=_params("parallel"),
        name="mla_latent",
    )(z, z, g.reshape(1, -1), ck, sk)


def _kv_proj_kernel(ckv_ref, kr_ref, wk_ref, wvt_ref, k_ref, vt_ref):
    c = ckv_ref[...].astype(BF16)
    kn = _dot(c, wk_ref[...])
    vt = _dot_nt(wvt_ref[...], c)
    kr = kr_ref[...].astype(BF16)
    for h in range(MLA_HEADS):
        k_ref[h, :, :MLA_NOPE] = kn[:, h * MLA_NOPE:(h + 1) * MLA_NOPE].astype(BF16)
        k_ref[h, :, MLA_NOPE:] = kr
        vt_ref[h, 0] = vt[h * MLA_V:(h + 1) * MLA_V].astype(BF16)


def kv_proj(ckv, kr, w_uk, w_uv_t, tile, rows):
    return pl.pallas_call(
        _kv_proj_kernel,
        grid=(rows // tile,),
        in_specs=[pl.BlockSpec((tile, MLA_KV_RANK), lambda i: (i, 0)),
                  pl.BlockSpec((tile, MLA_ROPE), lambda i: (i, 0)),
                  pl.BlockSpec(w_uk.shape, lambda i: (0, 0)),
                  pl.BlockSpec(w_uv_t.shape, lambda i: (0, 0))],
        out_specs=[pl.BlockSpec((MLA_HEADS, tile, MLA_QK), lambda i: (0, i, 0)),
                   pl.BlockSpec((MLA_HEADS, 1, MLA_V, tile), lambda i: (0, i, 0, 0))],
        out_shape=[jax.ShapeDtypeStruct((MLA_HEADS, rows, MLA_QK), BF16),
                   jax.ShapeDtypeStruct((MLA_HEADS, rows // tile, MLA_V, tile), BF16)],
        compiler_params=_params("parallel"),
        name="kv_proj",
    )(ckv, kr, w_uk, w_uv_t)


def _attn_prompt_kernel(qt_ref, k_ref, vt_ref, o_ref, sa_sc, sb_sc, m_sc, l_sc, acc_sc):
    i = pl.program_id(1)
    m_sc[...] = jnp.full(m_sc.shape, -jnp.inf, F32)
    l_sc[...] = jnp.zeros(l_sc.shape, F32)
    acc_sc[...] = jnp.zeros(acc_sc.shape, F32)

    def scores(j, s_sc):
        start = pl.multiple_of(j * ATT_T, ATT_T)
        for hb in range(ATT_HB):
            s_sc[hb] = _dot(k_ref[hb, pl.ds(start, ATT_T), :], qt_ref[hb])

    def step(j, s_sc, masked):
        for hb in range(ATT_HB):
            s = s_sc[hb]
            if masked:
                kc = lax.broadcasted_iota(jnp.int32, s.shape, 0) // CHUNK
                qc = lax.broadcasted_iota(jnp.int32, s.shape, 1) // CHUNK
                s = jnp.where(kc <= qc, s, -jnp.inf)
            m_prev = m_sc[hb]
            m_new = jnp.maximum(m_prev, jnp.max(s, axis=0, keepdims=True))
            alpha = jnp.exp2(m_prev - m_new)
            p = jnp.exp2(s - m_new)
            l_sc[hb] = alpha * l_sc[hb] + jnp.sum(p, axis=0, keepdims=True)
            acc_sc[hb] = alpha * acc_sc[hb] + _dot(vt_ref[hb, j], p.astype(BF16))
            m_sc[hb] = m_new

    def pair(jj, carry):
        j = 2 * jj
        scores(j + 1, sb_sc)
        step(j, sa_sc, False)
        scores(j + 2, sa_sc)
        step(j + 1, sb_sc, False)
        return carry

    scores(0, sa_sc)
    lax.fori_loop(0, i // 2, pair, 0)

    @pl.when(i % 2 == 0)
    def _():
        step(i, sa_sc, True)

    @pl.when(i % 2 == 1)
    def _():
        scores(i, sb_sc)
        step(i - 1, sa_sc, False)
        step(i, sb_sc, True)

    for hb in range(ATT_HB):
        out_t = acc_sc[hb] / l_sc[hb]
        o_ref[:, hb * MLA_V:(hb + 1) * MLA_V] = out_t.T.astype(o_ref.dtype)


def attn_prompt(qt, k, vt):
    nblk = SEQ // ATT_T
    return pl.pallas_call(
        _attn_prompt_kernel,
        grid=(MLA_HEADS // ATT_HB, nblk),
        in_specs=[pl.BlockSpec((ATT_HB, MLA_QK, ATT_T), lambda g, i: (g, 0, i)),
                  pl.BlockSpec((ATT_HB, SEQ, MLA_QK), lambda g, i: (g, 0, 0)),
                  pl.BlockSpec((ATT_HB, nblk, MLA_V, ATT_T), lambda g, i: (g, 0, 0, 0))],
        out_specs=pl.BlockSpec((ATT_T, ATT_HB * MLA_V), lambda g, i: (i, g)),
        out_shape=jax.ShapeDtypeStruct((ROWS, MLA_HEADS * MLA_V), BF16),
        scratch_shapes=[pltpu.VMEM((ATT_HB, ATT_T, ATT_T), F32), pltpu.VMEM((ATT_HB, ATT_T, ATT_T), F32),
                        pltpu.VMEM((ATT_HB, 1, ATT_T), F32), pltpu.VMEM((ATT_HB, 1, ATT_T), F32),
                        pltpu.VMEM((ATT_HB, MLA_V, ATT_T), F32)],
        compiler_params=_params("parallel", "arbitrary"),
        name="attn_prompt",
    )(qt, k, vt)


def _attn_sample_kernel(qt_ref, ckv_ref, kr_ref, wk_ref, wv_ref, into_ref, o_ref, qlat_sc, qrope_sc, olat_sc):
    del into_ref
    b = pl.program_id(0)
    pad_rope = jnp.zeros((LANES - MLA_ROPE, N_SAMPLE), F32)

    @pl.when(b == 0)
    def _():
        for h in range(MLA_HEADS):
            q_nope = qt_ref[h, :MLA_NOPE, :].astype(F32).T.astype(BF16)
            qlat_sc[h] = _dot_nt(q_nope, wk_ref[:, h * MLA_NOPE:(h + 1) * MLA_NOPE])
            q_rope = jnp.concatenate([qt_ref[h, MLA_NOPE:, :].astype(F32), pad_rope], axis=0)
            qrope_sc[h] = q_rope.T

    rows = pl.ds(pl.multiple_of(b * DEC_SEQ, DEC_SEQ), DEC_SEQ)
    q_lat = jnp.concatenate([qlat_sc[h, rows, :] for h in range(MLA_HEADS)], axis=0)
    q_rope = jnp.concatenate([qrope_sc[h, rows, :] for h in range(MLA_HEADS)], axis=0)[:, :MLA_ROPE]
    c = ckv_ref[0].astype(BF16)
    s = _dot_nt(q_lat.astype(BF16), c) + _dot_nt(q_rope.astype(BF16), kr_ref[0].astype(BF16))
    kpos = lax.broadcasted_iota(jnp.int32, s.shape, 1)
    qpos = PAST_LEN + lax.broadcasted_iota(jnp.int32, s.shape, 0) % DEC_SEQ
    kchunk = jnp.where(kpos < SAMPLE_KEYS, kpos // CHUNK, SAMPLE_KEYS_PAD)
    s = jnp.where(kchunk <= qpos // CHUNK, s, -jnp.inf)
    p = jnp.exp2(s - jnp.max(s, axis=-1, keepdims=True))
    o_lat = _dot(p.astype(BF16), c) / jnp.sum(p, axis=-1, keepdims=True)
    for h in range(MLA_HEADS):
        olat_sc[h, rows, :] = o_lat[h * DEC_SEQ:(h + 1) * DEC_SEQ]

    @pl.when(b == DEC_BATCH - 1)
    def _():
        for h in range(MLA_HEADS):
            o_h = _dot(olat_sc[h].astype(BF16), wv_ref[:, h * MLA_V:(h + 1) * MLA_V])
            o_ref[:, h * MLA_V:(h + 1) * MLA_V] = o_h.astype(o_ref.dtype)


def attn_sample(qt, ckv, kr, w_uk, w_uv, into):
    return pl.pallas_call(
        _attn_sample_kernel,
        grid=(DEC_BATCH,),
        in_specs=[pl.BlockSpec((MLA_HEADS, MLA_QK, N_SAMPLE), lambda b: (0, 0, SEQ // N_SAMPLE)),
                  pl.BlockSpec((1, SAMPLE_KEYS_PAD, MLA_KV_RANK), lambda b: (b, 0, 0)),
                  pl.BlockSpec((1, SAMPLE_KEYS_PAD, MLA_ROPE), lambda b: (b, 0, 0)),
                  pl.BlockSpec(w_uk.shape, lambda b: (0, 0)),
                  pl.BlockSpec(w_uv.shape, lambda b: (0, 0)),
                  pl.BlockSpec(memory_space=pl.ANY)],
        out_specs=pl.BlockSpec((N_SAMPLE, MLA_HEADS * MLA_V), lambda b: (SEQ // N_SAMPLE, 0)),
        out_shape=jax.ShapeDtypeStruct(into.shape, into.dtype),
        input_output_aliases={5: 0},
        scratch_shapes=[pltpu.VMEM((MLA_HEADS, N_SAMPLE, MLA_KV_RANK), F32),
                        pltpu.VMEM((MLA_HEADS, N_SAMPLE, LANES), F32),
                        pltpu.VMEM((MLA_HEADS, N_SAMPLE, MLA_KV_RANK), F32)],
        compiler_params=_params("arbitrary"),
        name="attn_sample",
    )(qt, ckv, kr, w_uk, w_uv, into)


CONV_PAD = 32
CONV_OFF = CONV_PAD - CONV_HIST


def _conv_kernel(za_ref, zb_ref, past_ref, w_ref, b_ref, g_ref, beta_ref, *rest, t_rows):
    o_ref, hist_ref, ext_sc = rest[-3:]

    @pl.when(pl.program_id(1) == 0)
    def _():
        ext_sc[0:SUBLANES, :] = jnp.zeros((SUBLANES, CONV_CH), F32)
        ext_sc[CONV_OFF:CONV_PAD, :] = past_ref[0]

    u = za_ref[...] * _sigmoid(zb_ref[...])
    ext_sc[CONV_PAD:CONV_PAD + t_rows, :] = u
    cols = []
    for c0 in range(0, CONV_CH, LANES):
        acc = None
        for shift in range(SUBLANES):
            win = t_rows + (SUBLANES if shift else 0)
            group = None
            for base in range(0, CONV_PAD + SUBLANES, SUBLANES):
                tap = base + shift - CONV_OFF
                if 0 <= tap < CONV_WIDTH:
                    term = w_ref[tap:tap + 1, c0:c0 + LANES] * ext_sc[base:base + win, c0:c0 + LANES]
                    group = term if group is None else group + term
            part = group[shift:shift + t_rows]
            acc = part if acc is None else acc + part
        cols.append(acc)
    y = jnp.concatenate(cols, axis=1) + b_ref[...]
    yc = y - jnp.mean(y, axis=-1, keepdims=True)
    y = yc * lax.rsqrt(jnp.mean(yc * yc, axis=-1, keepdims=True) + EPS) * g_ref[...] + beta_ref[...]
    o_ref[...] = (y * _sigmoid(y)).astype(o_ref.dtype)
    tail = ext_sc[t_rows + CONV_OFF:t_rows + CONV_PAD, :]
    hist_ref[0] = tail
    ext_sc[CONV_OFF:CONV_PAD, :] = tail


def _into_args(into):
    if into is None:
        return [], [], None
    return [pl.BlockSpec(memory_space=pl.ANY)], [into], into


def conv_mixer(z, past, w, b, g, beta, *, batch, seq, row_off, t_rows, into=None):
    steps = seq // t_rows
    first = row_off // t_rows

    def rows(bi, t):
        return first + bi * steps + t

    vec = pl.BlockSpec((1, CONV_CH), lambda bi, t: (0, 0))
    extra_specs, extra_args, alias = _into_args(into)
    return pl.pallas_call(
        functools.partial(_conv_kernel, t_rows=t_rows),
        grid=(batch, steps),
        in_specs=[pl.BlockSpec((t_rows, CONV_CH), lambda bi, t: (rows(bi, t), Z_CA // CONV_CH)),
                  pl.BlockSpec((t_rows, CONV_CH), lambda bi, t: (rows(bi, t), Z_CB // CONV_CH)),
                  pl.BlockSpec((1, CONV_HIST, CONV_CH), lambda bi, t: (bi, 0, 0)),
                  pl.BlockSpec((CONV_WIDTH, CONV_CH), lambda bi, t: (0, 0)),
                  vec, vec, vec] + extra_specs,
        out_specs=[pl.BlockSpec((t_rows, CONV_CH), lambda bi, t: (rows(bi, t), 0)),
                   pl.BlockSpec((1, CONV_HIST, CONV_CH), lambda bi, t: (bi, 0, 0))],
        out_shape=[jax.ShapeDtypeStruct((ROWS, CONV_CH), BF16),
                   jax.ShapeDtypeStruct((batch, CONV_HIST, CONV_CH), F32)],
        input_output_aliases={} if alias is None else {7: 0},
        scratch_shapes=[pltpu.VMEM((CONV_PAD + t_rows, CONV_CH), F32)],
        compiler_params=_params("parallel", "arbitrary"),
        name="conv_mixer",
    )(z, z, past, w, b.reshape(1, -1), g.reshape(1, -1), beta.reshape(1, -1), *extra_args)


def _pad_rows(x, rows):
    if x.shape[0] == rows:
        return x
    return jnp.concatenate([x, jnp.zeros((rows - x.shape[0], x.shape[1]), x.dtype)], axis=0)


def _log_sigmoid(x):
    return jnp.minimum(x, 0.0) - jnp.log1p(jnp.exp(-jnp.abs(x)))


def _gla_kernel(q_ref, k_ref, v_ref, r_ref, al_ref, wa_ref, ba_ref, gn_ref, s0_ref, *rest, c, cps):
    o_ref, s_out_ref, s_sc = rest[-3:]
    step = pl.program_id(1)

    @pl.when(step == 0)
    def _():
        s_sc[...] = s0_ref[0]

    row = lax.broadcasted_iota(jnp.int32, (GLA_CP, GLA_CP), 0)
    col = lax.broadcasted_iota(jnp.int32, (GLA_CP, GLA_CP), 1)
    causal = col <= row
    rows_p = cps * GLA_CP
    gate = _dot(al_ref[...].astype(BF16), wa_ref[...]) + ba_ref[...]
    log_a = _pad_rows(_log_sigmoid(gate) / GLA_TAU, rows_p)
    row_p = lax.broadcasted_iota(jnp.int32, (rows_p, rows_p), 0)
    col_p = lax.broadcasted_iota(jnp.int32, (rows_p, rows_p), 1)
    tri = jnp.where(col_p <= row_p, jnp.where(row_p // GLA_CP == col_p // GLA_CP, 1.0, 0.0), 0.0)
    bcum_all = jnp.dot(tri, log_a, precision=lax.Precision.HIGHEST, preferred_element_type=F32)
    grow_all = jnp.exp(bcum_all)
    shrink_all = jnp.exp(-bcum_all)
    for ci in range(cps):
        r0 = ci * c
        p0 = ci * GLA_CP
        for h in range(GLA_HEADS):
            ks = slice(h * GLA_DK, (h + 1) * GLA_DK)
            vs = slice(h * GLA_DV, (h + 1) * GLA_DV)
            b = bcum_all[p0:p0 + GLA_CP, ks]
            q = _pad_rows(q_ref[r0:r0 + c, ks], GLA_CP) * GLA_DK ** -0.5
            k = _pad_rows(k_ref[r0:r0 + c, ks], GLA_CP)
            v = _pad_rows(v_ref[r0:r0 + c, vs], GLA_CP).astype(BF16)
            q_t = (q * grow_all[p0:p0 + GLA_CP, ks]).astype(BF16)
            k_t = (k * shrink_all[p0:p0 + GLA_CP, ks]).astype(BF16)
            a = jnp.where(causal, _dot_nt(q_t, k_t), 0.0)
            s_prev = s_sc[h]
            o = _dot(q_t, s_prev.astype(BF16)) + _dot(a.astype(BF16), v)
            b_last = b[c - 1:c, :]
            k_dec = k * jnp.exp(b_last - b)
            k_dec_t = _pad_rows(k_dec, XPOSE).T
            decay = jnp.exp(_pad_rows(b, XPOSE).T[:, c - 1:c])
            s_sc[h] = decay * s_prev + _dot(k_dec_t.astype(BF16), _pad_rows(v, XPOSE))
            on = _rms(o[:c], gn_ref[...])
            r = r_ref[r0:r0 + c, vs]
            o_ref[r0:r0 + c, vs] = (on * (r * _sigmoid(r))).astype(o_ref.dtype)

    @pl.when(step == pl.num_programs(1) - 1)
    def _():
        s_out_ref[0] = s_sc[...]


def gla_mixer(z, s0, wa2, ba, gn, *, batch, seq, row_off, c, cps, into=None):
    t_rows = c * cps
    steps = seq // t_rows
    first = row_off // t_rows

    def rows(bi, t):
        return first + bi * steps + t

    state = pl.BlockSpec((1, GLA_HEADS, GLA_DK, GLA_DV), lambda bi, t: (bi, 0, 0, 0))
    extra_specs, extra_args, alias = _into_args(into)
    return pl.pallas_call(
        functools.partial(_gla_kernel, c=c, cps=cps),
        grid=(batch, steps),
        in_specs=[pl.BlockSpec((t_rows, HK), lambda bi, t: (rows(bi, t), Z_GQ // HK)),
                  pl.BlockSpec((t_rows, HK), lambda bi, t: (rows(bi, t), Z_GK // HK)),
                  pl.BlockSpec((t_rows, HV), lambda bi, t: (rows(bi, t), Z_GV // HV)),
                  pl.BlockSpec((t_rows, HV), lambda bi, t: (rows(bi, t), Z_GR // HV)),
                  pl.BlockSpec((t_rows, LANES), lambda bi, t: (rows(bi, t), Z_AL // LANES)),
                  pl.BlockSpec((LANES, HK), lambda bi, t: (0, 0)),
                  pl.BlockSpec((1, HK), lambda bi, t: (0, 0)),
                  pl.BlockSpec((1, GLA_DV), lambda bi, t: (0, 0)),
                  state] + extra_specs,
        out_specs=[pl.BlockSpec((t_rows, HV), lambda bi, t: (rows(bi, t), 0)), state],
        out_shape=[jax.ShapeDtypeStruct((ROWS, HV), BF16),
                   jax.ShapeDtypeStruct((batch, GLA_HEADS, GLA_DK, GLA_DV), F32)],
        input_output_aliases={} if alias is None else {9: 0},
        scratch_shapes=[pltpu.VMEM((GLA_HEADS, GLA_DK, GLA_DV), F32)],
        compiler_params=_params("parallel", "arbitrary"),
        name="gla_mixer",
    )(z, z, z, z, z, wa2, ba.reshape(1, -1), gn.reshape(1, -1), s0, *extra_args)


_SRC_CONV = MLA_Q_RANK + MLA_KV_RANK + MLA_ROPE
_SRC_GLA = _SRC_CONV + 2 * CONV_CH
D_IN = _SRC_GLA + 2 * HK + 2 * HV + GLA_GATE_RANK
_W_IN_PIECES = (
    (_SRC_CONV, CONV_CH, Z_CA), (_SRC_CONV + CONV_CH, CONV_CH, Z_CB),
    (_SRC_GLA + 2 * HK, HV, Z_GV), (_SRC_GLA + 2 * HK + HV, HV, Z_GR),
    (0, MLA_Q_RANK, Z_CQ),
    (_SRC_GLA, HK, Z_GQ), (_SRC_GLA + HK, HK, Z_GK),
    (MLA_Q_RANK, MLA_KV_RANK, Z_CKV),
    (MLA_Q_RANK + MLA_KV_RANK, MLA_ROPE, Z_KR),
    (_SRC_GLA + 2 * HK + 2 * HV, GLA_GATE_RANK, Z_AL),
)
PACK_SRC = 64


def _pack_tables():
    first, valid = [], []
    for col in range(0, Z_W, LANES):
        src, width, dst = next(p for p in _W_IN_PIECES if p[2] <= col < p[2] + max(p[1], LANES))
        first.append((src + col - dst) // PACK_SRC)
        valid.append(min(LANES, width))
    return jnp.asarray(first, jnp.int32), jnp.asarray(valid, jnp.int32)


def _pack_w_in_kernel(first_ref, valid_ref, lo_ref, hi_ref, o_ref):
    del first_ref
    x = jnp.concatenate([lo_ref[...], hi_ref[...]], axis=0)
    row = lax.broadcasted_iota(jnp.int32, x.shape, 0)
    x = jnp.where(row < valid_ref[pl.program_id(1)], x, 0.0)
    o_ref[...] = x.T.astype(BF16)


def pack_w_in(w_in_t):
    first, valid = _pack_tables()
    last = (D_IN - 1) // PACK_SRC
    return pl.pallas_call(
        _pack_w_in_kernel,
        grid_spec=pltpu.PrefetchScalarGridSpec(
            num_scalar_prefetch=2,
            grid=(DEPTH, Z_W // LANES),
            in_specs=[pl.BlockSpec((None, PACK_SRC, D_MODEL), lambda l, c, first, valid: (l, first[c], 0)),
                      pl.BlockSpec((None, PACK_SRC, D_MODEL),
                                   lambda l, c, first, valid: (l, jnp.minimum(first[c] + 1, last), 0))],
            out_specs=pl.BlockSpec((None, D_MODEL, LANES), lambda l, c, first, valid: (l, 0, c)),
        ),
        out_shape=jax.ShapeDtypeStruct((DEPTH, D_MODEL, Z_W), BF16),
        compiler_params=_params("parallel", "arbitrary"),
        name="pack_w_in",
    )(first, valid, w_in_t, w_in_t)


def _rope_tables():
    half = MLA_ROPE // 2
    pos = jnp.concatenate([jnp.arange(SEQ, dtype=jnp.int32),
                           jnp.tile(PAST_LEN + jnp.arange(DEC_SEQ, dtype=jnp.int32), DEC_BATCH)])
    inv_freq = ROPE_THETA ** (-jnp.arange(half, dtype=F32) / half)
    ang = pos.astype(F32)[:, None] * inv_freq[None, :]
    cos, sin = jnp.cos(ang), jnp.sin(ang)
    c64 = jnp.concatenate([cos, cos], axis=1)
    s64 = jnp.concatenate([-sin, sin], axis=1)
    zero = jnp.zeros_like(s64)
    key = (jnp.concatenate([c64, zero], axis=1), jnp.concatenate([s64, zero], axis=1))
    return (cos.T, sin.T), key


@jax.jit
def _forward(x_prompt, x_sample, cache_ckv, cache_krope, cache_conv, state_gla, p_prompt, p_sample, norm_mix,
             w_in, mla_q_norm, mla_kv_norm, mla_w_uq, mla_w_uk, mla_w_uv, conv_dw_w, conv_dw_b, conv_ln_g,
             conv_ln_b, gla_w_a2, gla_b_a, gla_norm, w_out, norm_ffn, ffn_w_gate, ffn_w_up, ffn_w_down,
             norm_ple, ple_w_gate, ple_w_proj, norm_final):
    w_in_p = pack_w_in(jnp.swapaxes(w_in, 1, 2))
    w_uq_t = mla_w_uq.astype(BF16).transpose(0, 2, 1)
    w_uk = mla_w_uk.reshape(DEPTH, MLA_KV_RANK, MLA_HEADS * MLA_NOPE).astype(BF16)
    w_uv = mla_w_uv.reshape(DEPTH, MLA_KV_RANK, MLA_HEADS * MLA_V).astype(BF16)
    w_uv_t = w_uv.transpose(0, 2, 1)
    w_a2 = jnp.concatenate([gla_w_a2.astype(BF16),
                            jnp.zeros((DEPTH, LANES - GLA_GATE_RANK, HK), BF16)], axis=1)
    w_out_b = w_out.astype(BF16)
    w_gate, w_up = ffn_w_gate, ffn_w_up
    w_down = ffn_w_down.astype(BF16)
    w_pg, w_pp = ple_w_gate.astype(BF16), ple_w_proj.astype(BF16)
    (cos_t, sin_t), (ck, sk) = _rope_tables()

    x = jnp.concatenate([x_prompt.reshape(SEQ, D_MODEL), x_sample.reshape(N_SAMPLE, D_MODEL)], axis=0)
    p_all = jnp.concatenate([p_prompt.reshape(DEPTH, SEQ, D_PLE), p_sample.reshape(DEPTH, N_SAMPLE, D_PLE)], axis=1)
    zero_conv = jnp.zeros((1, CONV_HIST, CONV_CH), F32)
    zero_gla = jnp.zeros((1, GLA_HEADS, GLA_DK, GLA_DV), F32)

    ckvs, krs, convs_p, convs_s, glas_p, glas_s = [], [], [], [], [], []
    xg, ss = norm_prep(x, norm_mix[0])
    for i in range(DEPTH):
        z = in_proj(xg, ss, w_in_p, i)

        qt = mla_q(z, mla_q_norm[i], w_uq_t[i], cos_t, sin_t)
        ckv, kr = mla_latent(z, mla_kv_norm[i], ck, sk)
        k_p, vt_p = kv_proj(ckv, kr, w_uk[i], w_uv_t[i], ATT_T, SEQ)
        pad = SAMPLE_KEYS_PAD - SAMPLE_KEYS
        ckv_s = jnp.concatenate([cache_ckv[i], ckv[SEQ:].reshape(DEC_BATCH, DEC_SEQ, MLA_KV_RANK),
                                 jnp.zeros((DEC_BATCH, pad, MLA_KV_RANK), F32)], axis=1)
        kr_s = jnp.concatenate([cache_krope[i], kr[SEQ:].reshape(DEC_BATCH, DEC_SEQ, MLA_ROPE),
                                jnp.zeros((DEC_BATCH, pad, MLA_ROPE), F32)], axis=1)
        o_a = attn_sample(qt, ckv_s, kr_s, w_uk[i], w_uv[i], into=attn_prompt(qt, k_p, vt_p))

        conv_args = (conv_dw_w[i], conv_dw_b[i], conv_ln_g[i], conv_ln_b[i])
        o_b, conv_p = conv_mixer(z, zero_conv, *conv_args, batch=1, seq=SEQ, row_off=0, t_rows=CONV_T)
        o_b, conv_s = conv_mixer(z, cache_conv[i], *conv_args, batch=DEC_BATCH, seq=DEC_SEQ, row_off=SEQ,
                                 t_rows=DEC_SEQ, into=o_b)

        gla_args = (w_a2[i], gla_b_a[i], gla_norm[i])
        o_c, gla_p = gla_mixer(z, zero_gla, *gla_args, batch=1, seq=SEQ, row_off=0, c=CHUNK, cps=GLA_CPS)
        o_c, gla_s = gla_mixer(z, state_gla[i], *gla_args, batch=DEC_BATCH, seq=DEC_SEQ, row_off=SEQ,
                               c=DEC_SEQ, cps=1, into=o_c)

        x, xg, ss = out_proj(o_a, o_b, o_c, w_out_b, x, norm_ffn[i], i)
        x, xg, ss = ffn_down(ffn_gate_up(xg, ss, w_gate, w_up, i), w_down, x, norm_ple[i], i)
        g_next = norm_mix[i + 1] if i + 1 < DEPTH else None
        x, xg, ss = ple_update(xg, ss, p_all, w_pg, w_pp, x, g_next, i)

        ckvs.append(ckv)
        krs.append(kr)
        convs_p.append(conv_p)
        convs_s.append(conv_s)
        glas_p.append(gla_p)
        glas_s.append(gla_s)

    y_prompt = rmsnorm_rows(x, norm_final, 0, SEQ, TY)
    y_sample = rmsnorm_rows(x, norm_final, SEQ, N_SAMPLE, N_SAMPLE)
    ckv_all, kr_all = jnp.stack(ckvs), jnp.stack(krs)
    return (y_prompt.reshape(1, SEQ, D_MODEL),
            y_sample.reshape(DEC_BATCH, DEC_SEQ, D_MODEL),
            ckv_all[:, :SEQ].reshape(DEPTH, 1, SEQ, MLA_KV_RANK),
            kr_all[:, :SEQ].reshape(DEPTH, 1, SEQ, MLA_ROPE),
            jnp.stack(convs_p),
            jnp.stack(glas_p),
            ckv_all[:, SEQ:].reshape(DEPTH, DEC_BATCH, DEC_SEQ, MLA_KV_RANK),
            kr_all[:, SEQ:].reshape(DEPTH, DEC_BATCH, DEC_SEQ, MLA_ROPE),
            jnp.stack(convs_s),
            jnp.stack(glas_s))


def kernel(x_prompt, x_sample, cache_ckv, cache_krope, cache_conv, state_gla, p_prompt, p_sample, norm_mix, w_in, mla_q_norm, mla_kv_norm, mla_w_uq, mla_w_uk, mla_w_uv, conv_dw_w, conv_dw_b, conv_ln_g, conv_ln_b, gla_w_a2, gla_b_a, gla_norm, w_out, norm_ffn, ffn_w_gate, ffn_w_up, ffn_w_down, norm_ple, ple_w_gate, ple_w_proj, norm_final):
    return _forward(x_prompt, x_sample, cache_ckv, cache_krope, cache_conv, state_gla, p_prompt, p_sample,
                    norm_mix, w_in, mla_q_norm, mla_kv_norm, mla_w_uq, mla_w_uk, mla_w_uv, conv_dw_w, conv_dw_b,
                    conv_ln_g, conv_ln_b, gla_w_a2, gla_b_a, gla_norm, w_out, norm_ffn, ffn_w_gate, ffn_w_up,
                    ffn_w_down, norm_ple, ple_w_gate, ple_w_proj, norm_final)
```
